```python
import math
import jax, jax.numpy as jnp
from jax import lax
import numpy as np

D_MODEL = 1024
BATCH = 8
SEQ = 2048
DEPTH = 4
DEC_BATCH = 128
DEC_SEQ = 4
PAST_LEN = 16384
PAGE_SIZE = 128

N_EVEN = (DEPTH + 1) // 2
N_ODD = DEPTH // 2
RET_HEADS = 4
RET_DK = 128
RET_DV = 128
RET_WIDTH = RET_HEADS * RET_DK
RET_CHUNK = 128
ROPE_BASE = 10000.0
CONV_DIM = D_MODEL // 2
CONV_WIDTH = 3
AB_IN = 4 * RET_WIDTH + 3 * CONV_DIM
AB_OUT = RET_HEADS * RET_DV + CONV_DIM
HG_HEADS = 8
HG_DK = D_MODEL // HG_HEADS
HG_DV = D_MODEL // HG_HEADS
HG_DIM = HG_HEADS * HG_DK
HG_CHUNK = 64
PEER_HEADS = 8
PEER_DK = 256
N_KEYS = 128
N_EXPERTS = N_KEYS * N_KEYS
PEER_TOPK = 16
PEER_BLOCK = 128
NORM_EPS = 1e-6

kernel_name = 'retconv_hgrn2_peer_adaln_step'


def _chunk_size(L, target):
    c = min(L, target)
    while L % c:
        c -= 1
    return c


def _rmsnorm(x, g):
    x32 = x.astype(jnp.float32)
    y = x32 * lax.rsqrt(jnp.mean(x32 * x32, axis=-1, keepdims=True) + NORM_EPS)
    return (y * g).astype(x.dtype)


def _rope(x, pos):
    half = x.shape[-1] // 2
    inv = 1.0 / (ROPE_BASE ** jnp.linspace(0.0, 1.0, half, dtype=jnp.float32))
    ang = pos[:, None] * inv[None, :]
    cos = jnp.cos(ang)[None, :, None, :]
    sin = jnp.sin(ang)[None, :, None, :]
    x = x.astype(jnp.float32)
    x1, x2 = x[..., :half], x[..., half:]
    return jnp.concatenate([x1 * cos - x2 * sin, x1 * sin + x2 * cos], axis=-1)


def _to_blocks(t, nc, chunk):
    N, L, H, d = t.shape
    return t.reshape(N, nc, chunk, H, d).transpose(1, 0, 3, 2, 4)


def _retention_chunked(q, k, v, S0, log_gamma, chunk):
    N, L, H, dk = q.shape
    dv = v.shape[-1]
    nc = L // chunk
    pos = jnp.arange(chunk, dtype=jnp.float32)
    rel = pos[:, None] - pos[None, :]
    causal = rel >= 0
    lg = log_gamma[:, None, None]
    decay = jnp.where(causal, jnp.exp(lg * jnp.where(causal, rel, 0.0)), 0.0)
    in_decay = jnp.exp(log_gamma[:, None] * (pos + 1.0))[..., None]
    st_decay = jnp.exp(log_gamma[:, None] * (chunk - 1.0 - pos))[..., None]
    ch_decay = jnp.exp(log_gamma * chunk)[:, None, None]

    def step(S, inp):
        qb, kb, vb = inp
        att = jnp.einsum('nhid,nhjd->nhij', qb, kb) * decay
        o = jnp.einsum('nhij,nhjv->nhiv', att, vb) + jnp.einsum('nhid,nhdv->nhiv', qb * in_decay, S)
        S = S * ch_decay + jnp.einsum('nhjd,nhjv->nhdv', kb * st_decay, vb)
        return S, o

    S, o = lax.scan(step, S0.astype(jnp.float32),
                    (_to_blocks(q, nc, chunk), _to_blocks(k, nc, chunk), _to_blocks(v, nc, chunk)))
    return o.transpose(1, 0, 3, 2, 4).reshape(N, L, H, dv), S


def _gla_chunked(q, k, v, logf, S0, chunk):
    N, L, H, dk = q.shape
    dv = v.shape[-1]
    nc = L // chunk
    causal = jnp.tril(jnp.ones((chunk, chunk), dtype=bool))[:, :, None]

    def step(S, inp):
        qb, kb, vb, gb = inp
        b = jnp.cumsum(gb, axis=2)
        diff = b[:, :, :, None, :] - b[:, :, None, :, :]
        dec = jnp.exp(jnp.where(causal, diff, -jnp.inf))
        att = jnp.einsum('nhid,nhjd,nhijd->nhij', qb, kb, dec)
        o = jnp.einsum('nhij,nhjv->nhiv', att, vb) + jnp.einsum('nhid,nhdv->nhiv', qb * jnp.exp(b), S)
        b_last = b[:, :, -1:, :]
        S = S * jnp.exp(b_last[:, :, 0, :, None]) + jnp.einsum('nhjd,nhjv->nhdv', kb * jnp.exp(b_last - b), vb)
        return S, o

    S, o = lax.scan(step, S0.astype(jnp.float32),
                    (_to_blocks(q, nc, chunk), _to_blocks(k, nc, chunk),
                     _to_blocks(v, nc, chunk), _to_blocks(logf, nc, chunk)))
    return o.transpose(1, 0, 3, 2, 4).reshape(N, L, H, dv), S


def _retention_conv_mixer(h, pos, S0, zprev, w_in, ret_g, conv_w, w_out):
    N, L, _ = h.shape
    dt = h.dtype
    R, C = RET_WIDTH, CONV_DIM
    proj = h @ w_in
    q, k, v, g, bg, cg, xin = jnp.split(proj, [R, 2 * R, 3 * R, 4 * R, 4 * R + C, 4 * R + 2 * C], axis=-1)
    q = _rope(q.reshape(N, L, RET_HEADS, RET_DK), pos)
    k = _rope(k.reshape(N, L, RET_HEADS, RET_DK), pos) * (RET_DK ** -0.5)
    v = v.reshape(N, L, RET_HEADS, RET_DV).astype(jnp.float32)
    log_gamma = jnp.log(1.0 - 2.0 ** (-5.0 - jnp.arange(RET_HEADS, dtype=jnp.float32)))
    o, S = _retention_chunked(q, k, v, S0, log_gamma, _chunk_size(L, RET_CHUNK))
    o = _rmsnorm(o, ret_g).reshape(N, L, RET_HEADS * RET_DV)
    ret_out = (jax.nn.silu(g.astype(jnp.float32)) * o).astype(dt)
    z = cg * xin
    zpad = jnp.concatenate([zprev.astype(dt), z], axis=1)
    conv = sum(conv_w[i] * zpad[:, i:i + L] for i in range(CONV_WIDTH))
    conv_out = bg * conv
    y = jnp.concatenate([ret_out, conv_out], axis=-1) @ w_out
    return y, S.astype(dt), zpad[:, -(CONV_WIDTH - 1):]


def _hgrn2_mixer(h, S0, layer, w_in, lower_bounds, norm_g, w_out):
    N, L, _ = h.shape
    dt = h.dtype
    proj = h @ w_in
    q, f, i, g = jnp.split(proj, 4, axis=-1)
    q = jax.nn.silu(q.astype(jnp.float32)).reshape(N, L, HG_HEADS, HG_DK)
    sm = jax.nn.softmax(lower_bounds.astype(jnp.float32), axis=0)
    lb = (jnp.cumsum(sm, axis=0) - sm[0])[layer]
    fg = lb + (1.0 - lb) * jax.nn.sigmoid(f.astype(jnp.float32))
    logf = jnp.log(fg).reshape(N, L, HG_HEADS, HG_DK)
    kk = (1.0 - fg).reshape(N, L, HG_HEADS, HG_DK)
    iv = i.astype(jnp.float32).reshape(N, L, HG_HEADS, HG_DV)
    o, S = _gla_chunked(q, kk, iv, logf, S0, _chunk_size(L, HG_CHUNK))
    o = _rmsnorm(o, norm_g).reshape(N, L, HG_DIM) * jax.nn.silu(g.astype(jnp.float32))
    return o.astype(dt) @ w_out, S.astype(dt)


def _peer(h, wq, keys, u, v):
    N, L, D = h.shape
    blk = _chunk_size(L, PEER_BLOCK)
    hb_all = h.reshape(-1, blk, D)

    def block(hb):
        T = hb.shape[0]
        q = (hb @ wq).reshape(T, PEER_HEADS, 2, PEER_DK // 2)
        s = jnp.einsum('thpc,hpkc->thpk', q, keys).astype(jnp.float32)
        v1, i1 = lax.top_k(s[:, :, 0], PEER_TOPK)
        v2, i2 = lax.top_k(s[:, :, 1], PEER_TOPK)
        comb = (v1[..., :, None] + v2[..., None, :]).reshape(T, PEER_HEADS, PEER_TOPK * PEER_TOPK)
        vals, sel = lax.top_k(comb, PEER_TOPK)
        e1 = jnp.take_along_axis(i1, sel // PEER_TOPK, axis=-1)
        e2 = jnp.take_along_axis(i2, sel % PEER_TOPK, axis=-1)
        eid = e1 * N_KEYS + e2
        gate = jax.nn.softmax(vals, axis=-1)
        act = jax.nn.gelu(jnp.einsum('thkd,td->thk', u[eid], hb).astype(jnp.float32), approximate=False)
        w = (gate * act).astype(hb.dtype)
        return jnp.einsum('thk,thkd->td', w, v[eid])

    return lax.map(block, hb_all).reshape(N, L, D)


def _trunk(x, c, pos0, ret_S, conv_S, hg_S, p):
    N, L, _ = x.shape
    pos = pos0 + jnp.arange(L, dtype=jnp.float32)
    new_ret, new_conv, new_hg = [], [], []
    for l in range(DEPTH):
        mod = (jax.nn.silu(c) @ p['ada_w'][l] + p['ada_b'][l])[:, None, :]
        sh1, sc1, g1, sh2, sc2, g2 = jnp.split(mod, 6, axis=-1)
        h = _rmsnorm(x, p['norm_mix_g'][l]) * (1.0 + sc1) + sh1
        j = l // 2
        if l % 2 == 0:
            y, s_r, s_c = _retention_conv_mixer(h, pos, ret_S[j], conv_S[j], p['ab_w_in'][j],
                                                p['ret_norm_g'][j], p['conv_w'][j], p['ab_w_out'][j])
            new_ret.append(s_r)
            new_conv.append(s_c)
        else:
            y, s_h = _hgrn2_mixer(h, hg_S[j], l, p['hg_w_in'][j], p['hg_lower_bounds'],
                                  p['hg_norm_g'][j], p['hg_w_out'][j])
            new_hg.append(s_h)
        x = x + g1 * y
        h = _rmsnorm(x, p['norm_ffn_g'][l]) * (1.0 + sc2) + sh2
        x = x + g2 * _peer(h, p['peer_wq'][l], p['peer_keys'][l], p['peer_u'][l], p['peer_v'][l])
    y = _rmsnorm(x, p['final_norm_g'])
    return y, jnp.stack(new_ret), jnp.stack(new_conv), jnp.stack(new_hg)


def setup_inputs(seed: int = 0) -> dict:
    key = jax.random.key(seed)
    ks = jax.random.split(key, 24)
    D = D_MODEL

    def nrm(k, shape, s):
        return jax.random.normal(k, shape, jnp.float32) * s

    return {
        'x_prompt': nrm(ks[0], (BATCH, SEQ, D), 1.0),
        'x_sample': nrm(ks[1], (DEC_BATCH, DEC_SEQ, D), 1.0),
        'state_ret': nrm(ks[2], (N_EVEN, DEC_BATCH, RET_HEADS, RET_DK, RET_DV), 0.5),
        'state_conv': nrm(ks[3], (N_EVEN, DEC_BATCH, CONV_WIDTH - 1, CONV_DIM), 1.0),
        'state_hgrn': nrm(ks[4], (N_ODD, DEC_BATCH, HG_HEADS, HG_DK, HG_DV), 0.5),
        'c_prompt': nrm(ks[5], (BATCH, D), 1.0),
        'c_sample': nrm(ks[6], (DEC_BATCH, D), 1.0),
        'ada_w': nrm(ks[7], (DEPTH, D, 6 * D), 0.5 * D ** -0.5),
        'ada_b': nrm(ks[8], (DEPTH, 6 * D), 0.02),
        'norm_mix_g': 1.0 + nrm(ks[9], (DEPTH, D), 0.02),
        'norm_ffn_g': 1.0 + nrm(ks[10], (DEPTH, D), 0.02),
        'ab_w_in': nrm(ks[11], (N_EVEN, D, AB_IN), D ** -0.5),
        'ret_norm_g': 1.0 + nrm(ks[12], (N_EVEN, RET_HEADS, RET_DV), 0.02),
        'conv_w': nrm(ks[13], (N_EVEN, CONV_WIDTH, CONV_DIM), CONV_WIDTH ** -0.5),
        'ab_w_out': nrm(ks[14], (N_EVEN, AB_OUT, D), AB_OUT ** -0.5),
        'hg_w_in': nrm(ks[15], (N_ODD, D, 4 * HG_DIM), D ** -0.5),
        'hg_lower_bounds': nrm(ks[16], (DEPTH, HG_DIM), 0.1),
        'hg_norm_g': 1.0 + nrm(ks[17], (N_ODD, HG_HEADS, HG_DV), 0.02),
        'hg_w_out': nrm(ks[18], (N_ODD, HG_DIM, D), HG_DIM ** -0.5),
        'peer_wq': nrm(ks[19], (DEPTH, D, PEER_HEADS * PEER_DK), D ** -0.5),
        'peer_keys': nrm(ks[20], (DEPTH, PEER_HEADS, 2, N_KEYS, PEER_DK // 2), (PEER_DK // 2) ** -0.5),
        'peer_u': nrm(ks[21], (DEPTH, N_EXPERTS, D), D ** -0.5),
        'peer_v': nrm(ks[22], (DEPTH, N_EXPERTS, D), 0.5),
        'final_norm_g': 1.0 + nrm(ks[23], (D,), 0.02),
    }


def reference(x_prompt, x_sample, state_ret, state_conv, state_hgrn, c_prompt, c_sample,
              ada_w, ada_b, norm_mix_g, norm_ffn_g, ab_w_in, ret_norm_g, conv_w, ab_w_out,
              hg_w_in, hg_lower_bounds, hg_norm_g, hg_w_out, peer_wq, peer_keys, peer_u, peer_v,
              final_norm_g):
    p = dict(ada_w=ada_w, ada_b=ada_b, norm_mix_g=norm_mix_g, norm_ffn_g=norm_ffn_g,
             ab_w_in=ab_w_in, ret_norm_g=ret_norm_g, conv_w=conv_w, ab_w_out=ab_w_out,
             hg_w_in=hg_w_in, hg_lower_bounds=hg_lower_bounds, hg_norm_g=hg_norm_g, hg_w_out=hg_w_out,
             peer_wq=peer_wq, peer_keys=peer_keys, peer_u=peer_u, peer_v=peer_v,
             final_norm_g=final_norm_g)
    nb = x_prompt.shape[0]
    dt = x_prompt.dtype
    ret0 = jnp.zeros((N_EVEN, nb, RET_HEADS, RET_DK, RET_DV), dt)
    conv0 = jnp.zeros((N_EVEN, nb, CONV_WIDTH - 1, CONV_DIM), dt)
    hg0 = jnp.zeros((N_ODD, nb, HG_HEADS, HG_DK, HG_DV), dt)
    y_prompt, ret_p, conv_p, hg_p = _trunk(x_prompt, c_prompt, 0.0, ret0, conv0, hg0, p)
    y_sample, ret_s, conv_s, hg_s = _trunk(x_sample, c_sample, float(PAST_LEN),
                                           state_ret, state_conv, state_hgrn, p)
    return (y_prompt, y_sample, ret_p, conv_p, hg_p, ret_s, conv_s, hg_s)
```

```python
import functools
import math

import jax
import jax.numpy as jnp
from jax import lax
from jax.experimental import pallas as pl
from jax.experimental.pallas import tpu as pltpu

F32 = jnp.float32
BF16 = jnp.bfloat16
I32 = jnp.int32
U32 = jnp.uint32

D_MODEL = 1024
DEPTH = 4
PAST_LEN = 16384
RET_HEADS = 4
RET_DK = 128
RET_WIDTH = RET_HEADS * RET_DK
RET_CHUNK = 128
ROPE_BASE = 10000.0
CONV_DIM = D_MODEL // 2
HG_HEADS = 8
HG_DK = 128
HG_SUB = 16
HG_BLOCK = 256
PEER_HEADS = 8
N_KEYS = 128
PEER_TOPK = 16
NORM_EPS = 1e-6

LANES = 128
SUBLANES = 8
VMEM_LIMIT = 56 * 1024 * 1024

TOKEN_BLOCK = 512
ROUTE_BLOCK = 256
EXPERT_ROWS = 512
EXPERT_CHUNK = 2048
W_PITCH = EXPERT_ROWS + SUBLANES


def _cparams(sem):
    return pltpu.CompilerParams(dimension_semantics=sem, vmem_limit_bytes=VMEM_LIMIT)


def _mod_spec(per_token, tm, rows_per_seq):
    if per_token:
        return pl.BlockSpec((tm, D_MODEL), lambda i, *_: (i, 0))
    return pl.BlockSpec((None, 1, D_MODEL), lambda i, *_: (i * tm // rows_per_seq, 0, 0))


def _norm_mod(x, g, sc, sh):
    y = x * lax.rsqrt(jnp.mean(x * x, axis=-1, keepdims=True) + NORM_EPS)
    return (y * g) * (1.0 + sc) + sh


def _silu(x):
    return x * (1.0 / (1.0 + jnp.exp(-x)))


def _sigmoid(x):
    return 1.0 / (1.0 + jnp.exp(-x))


def _ada_kernel(c_ref, w_ref, b_ref, o_ref):
    a = _silu(c_ref[...]).astype(BF16)
    o_ref[...] = jnp.dot(a, w_ref[...].astype(BF16), preferred_element_type=F32) + b_ref[...]


def _ada(c, ada_w, ada_b):
    n = c.shape[0]
    tn = 1536
    nj = 6 * D_MODEL // tn
    return pl.pallas_call(
        _ada_kernel,
        grid=(DEPTH, nj),
        in_specs=[
            pl.BlockSpec((n, D_MODEL), lambda l, j: (0, 0)),
            pl.BlockSpec((None, D_MODEL, tn), lambda l, j: (l, 0, j)),
            pl.BlockSpec((None, 1, tn), lambda l, j: (l, 0, j)),
        ],
        out_specs=pl.BlockSpec((None, n, tn), lambda l, j: (l, 0, j)),
        out_shape=jax.ShapeDtypeStruct((DEPTH, n, 6 * D_MODEL), F32),
        compiler_params=_cparams(("parallel", "parallel")),
        name="ada_mod",
    )(c, ada_w, ada_b.reshape(DEPTH, 1, 6 * D_MODEL))


def _nmm_kernel(x_ref, g_ref, sc_ref, sh_ref, w_ref, o_ref, h_ref):
    @pl.when(pl.program_id(1) == 0)
    def _():
        h_ref[...] = _norm_mod(x_ref[...], g_ref[...], sc_ref[...], sh_ref[...]).astype(BF16)

    o_ref[...] = jnp.dot(h_ref[...], w_ref[...], preferred_element_type=F32)


def _norm_mod_matmul(x, g, sc, sh, w, per_token, rows_per_seq):
    t = x.shape[0]
    f = w.shape[1]
    tm = min(TOKEN_BLOCK, t)
    tn = 512
    mspec = _mod_spec(per_token, tm, rows_per_seq)
    return pl.pallas_call(
        _nmm_kernel,
        grid=(t // tm, f // tn),
        in_specs=[
            pl.BlockSpec((tm, D_MODEL), lambda i, j: (i, 0)),
            pl.BlockSpec((1, D_MODEL), lambda i, j: (0, 0)),
            mspec, mspec,
            pl.BlockSpec((D_MODEL, tn), lambda i, j: (0, j)),
        ],
        out_specs=pl.BlockSpec((tm, tn), lambda i, j: (i, j)),
        out_shape=jax.ShapeDtypeStruct((t, f), F32),
        scratch_shapes=[pltpu.VMEM((tm, D_MODEL), BF16)],
        compiler_params=_cparams(("parallel", "arbitrary")),
        name="norm_mod_proj",
    )(x, g.reshape(1, D_MODEL), sc, sh, w)


def _ret_kernel(lg_ref, q_ref, k_ref, v_ref, g_ref, cos_ref, sin_ref, rg_ref, s0_ref,
                o_ref, sout_ref, s_scr, *, chunk, real):
    c = pl.program_id(2)

    @pl.when(c == 0)
    def _():
        s_scr[...] = s0_ref[...]

    lg = lg_ref[pl.program_id(1)]
    cos = cos_ref[...]
    sin = sin_ref[...]

    def rope(x):
        return x * cos + pltpu.roll(x, RET_DK // 2, 1) * sin

    q = rope(q_ref[...])
    k = rope(k_ref[...]) * (RET_DK ** -0.5)
    v = v_ref[...]
    ri = lax.broadcasted_iota(I32, (chunk, chunk), 0)
    ci = lax.broadcasted_iota(I32, (chunk, chunk), 1)
    causal = ri >= ci
    rel = jnp.where(causal, ri - ci, 0).astype(F32)
    decay = jnp.where(causal, jnp.exp(lg * rel), 0.0)
    pos = lax.broadcasted_iota(I32, (chunk, RET_DK), 0).astype(F32)
    in_decay = jnp.exp(lg * (pos + 1.0))
    st_decay = jnp.exp(lg * (real - 1.0 - pos))
    ch_decay = jnp.exp(jnp.full((1, RET_DK), real, F32) * lg)

    s = s_scr[...]
    att = lax.dot_general(q, k, (((1,), (1,)), ((), ())), preferred_element_type=F32) * decay
    o = (jnp.dot(att, v, preferred_element_type=F32)
         + jnp.dot(q * in_decay, s, preferred_element_type=F32))
    s_new = s * ch_decay + lax.dot_general(k * st_decay, v, (((0,), (0,)), ((), ())),
                                           preferred_element_type=F32)
    s_scr[...] = s_new

    y = o * lax.rsqrt(jnp.mean(o * o, axis=-1, keepdims=True) + NORM_EPS) * rg_ref[...]
    o_ref[...] = _silu(g_ref[...]) * y

    @pl.when(c == pl.num_programs(2) - 1)
    def _():
        sout_ref[...] = s_new


def _retention(proj, cos, sin, log_gamma, ret_g, s0, n, rows, real):
    chunk = min(rows, RET_CHUNK)
    nc = rows // chunk
    h = RET_HEADS

    def col(off):
        return pl.BlockSpec((chunk, RET_DK), lambda b, hh, c, *_: (b * nc + c, off + hh))

    tab = pl.BlockSpec((chunk, RET_DK), lambda b, hh, c, *_: (c, 0))
    st = pl.BlockSpec((None, None, RET_DK, RET_DK), lambda b, hh, c, *_: (b, hh, 0, 0))
    grid_spec = pltpu.PrefetchScalarGridSpec(
        num_scalar_prefetch=1,
        grid=(n, h, nc),
        in_specs=[col(0), col(h), col(2 * h), col(3 * h), tab, tab,
                  pl.BlockSpec((None, 1, RET_DK), lambda b, hh, c, *_: (hh, 0, 0)), st],
        out_specs=[pl.BlockSpec((chunk, RET_DK), lambda b, hh, c, *_: (b * nc + c, hh)), st],
        scratch_shapes=[pltpu.VMEM((RET_DK, RET_DK), F32)],
    )
    return pl.pallas_call(
        functools.partial(_ret_kernel, chunk=chunk, real=float(real if nc == 1 else chunk)),
        grid_spec=grid_spec,
        out_shape=[jax.ShapeDtypeStruct((n * rows, RET_WIDTH), F32),
                   jax.ShapeDtypeStruct((n, h, RET_DK, RET_DK), F32)],
        compiler_params=_cparams(("parallel", "parallel", "arbitrary")),
        name="retention",
    )(log_gamma, proj, proj, proj, proj, cos, sin, ret_g.reshape(h, 1, RET_DK), s0)


def _conv_kernel(bg_ref, cg_ref, xi_ref, w_ref, zp_ref, o_ref, tail_ref, prev_scr, *, chunk):
    c = pl.program_id(1)

    @pl.when(c == 0)
    def _():
        prev_scr[...] = zp_ref[...]

    z = cg_ref[...] * xi_ref[...]
    p2 = prev_scr[SUBLANES - 2:SUBLANES - 1, :]
    p1 = prev_scr[SUBLANES - 1:SUBLANES, :]
    row = lax.broadcasted_iota(I32, (chunk, CONV_DIM), 0)
    z1 = jnp.where(row == 0, p1, pltpu.roll(z, 1, 0))
    z2 = jnp.where(row == 0, p2, jnp.where(row == 1, p1, pltpu.roll(z, 2, 0)))
    conv = w_ref[0:1, :] * z2 + w_ref[1:2, :] * z1 + w_ref[2:3, :] * z
    o_ref[...] = bg_ref[...] * conv
    tail = z[chunk - SUBLANES:, :]
    prev_scr[...] = tail
    tail_ref[...] = tail


def _conv(proj, conv_w, zprev8, n, rows):
    chunk = min(rows, TOKEN_BLOCK)
    nc = rows // chunk
    base = 4 * RET_WIDTH // CONV_DIM

    def col(j):
        return pl.BlockSpec((chunk, CONV_DIM), lambda b, c: (b * nc + c, base + j))

    return pl.pallas_call(
        functools.partial(_conv_kernel, chunk=chunk),
        grid=(n, nc),
        in_specs=[col(0), col(1), col(2),
                  pl.BlockSpec((3, CONV_DIM), lambda b, c: (0, 0)),
                  pl.BlockSpec((None, SUBLANES, CONV_DIM), lambda b, c: (b, 0, 0))],
        out_specs=[pl.BlockSpec((chunk, CONV_DIM), lambda b, c: (b * nc + c, 0)),
                   pl.BlockSpec((None, SUBLANES, CONV_DIM), lambda b, c: (b, 0, 0))],
        out_shape=[jax.ShapeDtypeStruct((n * rows, CONV_DIM), F32),
                   jax.ShapeDtypeStruct((n, SUBLANES, CONV_DIM), F32)],
        scratch_shapes=[pltpu.VMEM((SUBLANES, CONV_DIM), F32)],
        compiler_params=_cparams(("parallel", "arbitrary")),
        name="short_conv",
    )(proj, proj, proj, conv_w, zprev8)


def _hgrn_kernel(q_ref, f_ref, i_ref, g_ref, lb_ref, ng_ref, s0_ref, o_ref, sout_ref,
                 st_scr, b_scr, q_scr, k_scr, o_scr, *, block, sub, real):
    c = pl.program_id(2)

    @pl.when(c == 0)
    def _():
        st_scr[...] = s0_ref[...].T

    lb = lb_ref[...]
    fg = lb + (1.0 - lb) * _sigmoid(f_ref[...])
    k_scr[...] = 1.0 - fg
    q_scr[...] = _silu(q_ref[...])
    ri = lax.broadcasted_iota(I32, (block, block), 0)
    ci = lax.broadcasted_iota(I32, (block, block), 1)
    tri = jnp.where((ci <= ri) & ((ci // sub) == (ri // sub)), 1.0, 0.0).astype(F32)
    b_scr[...] = jnp.dot(tri, jnp.log(fg), preferred_element_type=F32,
                         precision=lax.Precision.HIGHEST)

    ones = jnp.ones((HG_DK, HG_DK), BF16)
    rowi = lax.broadcasted_iota(I32, (sub, HG_DK), 0)
    nsub = block // sub
    last = (real - 1) if nsub == 1 else (sub - 1)

    def body(sb, carry):
        r0 = pl.multiple_of(sb * sub, sub)
        rows = pl.ds(r0, sub)
        b = b_scr[rows, :]
        q = q_scr[rows, :]
        kk = k_scr[rows, :]
        iv = i_ref[rows, :]
        st = st_scr[...]
        o = lax.dot_general((q * jnp.exp(b)).astype(BF16), st.astype(BF16),
                            (((1,), (1,)), ((), ())), preferred_element_type=F32)
        parts = []
        for j in range(sub):
            e = jnp.exp(jnp.where(rowi >= j, b - b[j:j + 1, :], -jnp.inf))
            parts.append(q * kk[j:j + 1, :] * e)
        t_all = jnp.concatenate(parts, axis=0).astype(BF16)
        s_all = jnp.dot(t_all, ones, preferred_element_type=F32)
        for j in range(sub):
            o = o + s_all[j * sub:(j + 1) * sub, :] * iv[j:j + 1, :]
        b_end = b[last:last + 1, :]
        kd = (kk * jnp.exp(b_end - b)).astype(BF16)
        st_scr[...] = st * jnp.exp(b_end) + lax.dot_general(
            iv.astype(BF16), kd, (((0,), (0,)), ((), ())), preferred_element_type=F32)
        o_scr[rows, :] = o
        return carry

    lax.fori_loop(0, nsub, body, 0)

    o = o_scr[...]
    y = o * lax.rsqrt(jnp.mean(o * o, axis=-1, keepdims=True) + NORM_EPS) * ng_ref[...]
    o_ref[...] = y * _silu(g_ref[...])

    @pl.when(c == pl.num_programs(2) - 1)
    def _():
        sout_ref[...] = st_scr[...].T


def _hgrn(proj, lb, norm_g, s0, n, rows, real):
    block = min(rows, HG_BLOCK)
    nc = rows // block
    sub = min(HG_SUB, block)
    h = HG_HEADS

    def col(off):
        return pl.BlockSpec((block, HG_DK), lambda b, hh, c: (b * nc + c, off + hh))

    vec = pl.BlockSpec((None, 1, HG_DK), lambda b, hh, c: (hh, 0, 0))
    st = pl.BlockSpec((None, None, HG_DK, HG_DK), lambda b, hh, c: (b, hh, 0, 0))
    return pl.pallas_call(
        functools.partial(_hgrn_kernel, block=block, sub=sub, real=real),
        grid=(n, h, nc),
        in_specs=[col(0), col(h), col(2 * h), col(3 * h), vec, vec, st],
        out_specs=[pl.BlockSpec((block, HG_DK), lambda b, hh, c: (b * nc + c, hh)), st],
        out_shape=[jax.ShapeDtypeStruct((n * rows, h * HG_DK), F32),
                   jax.ShapeDtypeStruct((n, h, HG_DK, HG_DK), F32)],
        scratch_shapes=[pltpu.VMEM((HG_DK, HG_DK), F32)] + [pltpu.VMEM((block, HG_DK), F32)] * 4,
        compiler_params=_cparams(("parallel", "parallel", "arbitrary")),
        name="hgrn2",
    )(proj, proj, proj, proj, lb.reshape(h, 1, HG_DK), norm_g.reshape(h, 1, HG_DK), s0)


def _outproj_kernel(*refs, n_in):
    x_ref, gate_ref = refs[0], refs[1]
    a_refs = refs[2:2 + n_in]
    w_refs = refs[2 + n_in:2 + 2 * n_in]
    o_ref = refs[2 + 2 * n_in]
    y = jnp.dot(a_refs[0][...].astype(BF16), w_refs[0][...], preferred_element_type=F32)
    for a_ref, w_ref in zip(a_refs[1:], w_refs[1:]):
        y = y + jnp.dot(a_ref[...].astype(BF16), w_ref[...], preferred_element_type=F32)
    o_ref[...] = x_ref[...] + gate_ref[...] * y


def _outproj(x, gate, acts, ws, per_token, rows_per_seq):
    t = x.shape[0]
    tm = min(TOKEN_BLOCK, t)
    n_in = len(acts)
    in_specs = [pl.BlockSpec((tm, D_MODEL), lambda i: (i, 0)), _mod_spec(per_token, tm, rows_per_seq)]
    in_specs += [pl.BlockSpec((tm, a.shape[1]), lambda i: (i, 0)) for a in acts]
    in_specs += [pl.BlockSpec(w.shape, lambda i: (0, 0)) for w in ws]
    return pl.pallas_call(
        functools.partial(_outproj_kernel, n_in=n_in),
        grid=(t // tm,),
        in_specs=in_specs,
        out_specs=pl.BlockSpec((tm, D_MODEL), lambda i: (i, 0)),
        out_shape=jax.ShapeDtypeStruct((t, D_MODEL), F32),
        compiler_params=_cparams(("parallel",)),
        name="out_proj",
    )(x, gate, *acts, *ws)


def _topk16(v_ref, code_fn, val_out, code_out):
    big = jnp.int32(2 ** 31 - 1)

    def body(k, carry):
        v = v_ref[...]
        code = code_fn()
        m = jnp.max(v, axis=0, keepdims=True)
        cmin = jnp.min(jnp.where(v == m, code, big), axis=0, keepdims=True)
        v_ref[...] = jnp.where(code == cmin, -jnp.inf, v)
        val_out[pl.ds(k, 1), :] = m
        code_out[pl.ds(k, 1), :] = cmin
        return carry

    lax.fori_loop(0, PEER_TOPK, body, 0)


def _route_kernel(x_ref, g_ref, sc_ref, sh_ref, wq_ref, keys_ref, e1_ref, e2_ref, gt_ref,
                  q_scr, s_scr, comb_scr, code_scr, v1_scr, i1_scr, v2_scr, i2_scr,
                  vs_scr, cs_scr, e1t_scr, e2t_scr, gtt_scr, *, tr):
    h = _norm_mod(x_ref[...], g_ref[...], sc_ref[...], sh_ref[...]).astype(BF16)
    q = jnp.dot(h, wq_ref[...], preferred_element_type=F32).astype(BF16)
    for hp in range(2 * PEER_HEADS):
        q_scr[hp] = q[:, hp * N_KEYS:(hp + 1) * N_KEYS]

    kk = PEER_TOPK

    def head(hh, carry):
        for p, (v_out, i_out) in enumerate(((v1_scr, i1_scr), (v2_scr, i2_scr))):
            s_scr[...] = lax.dot_general(keys_ref[2 * hh + p], q_scr[2 * hh + p],
                                         (((1,), (1,)), ((), ())), preferred_element_type=F32)
            _topk16(s_scr, lambda: lax.broadcasted_iota(I32, (N_KEYS, tr), 0), v_out, i_out)
        v2 = v2_scr[...]
        i2 = i2_scr[...]
        pos = lax.broadcasted_iota(I32, (kk, tr), 0)
        for a in range(kk):
            comb_scr[a * kk:(a + 1) * kk, :] = v1_scr[a:a + 1, :] + v2
            code_scr[a * kk:(a + 1) * kk, :] = (((pos + a * kk) << 14)
                                                 | (i1_scr[a:a + 1, :] * N_KEYS + i2))
        _topk16(comb_scr, lambda: code_scr[...], vs_scr, cs_scr)
        vals = vs_scr[...]
        ex = jnp.exp(vals - vals[0:1, :])
        gate = ex / jnp.sum(ex, axis=0, keepdims=True)
        eid = cs_scr[...] & (N_KEYS * N_KEYS - 1)
        rows = pl.ds(pl.multiple_of(hh * kk, kk), kk)
        e1t_scr[rows, :] = eid >> 7
        e2t_scr[rows, :] = eid & (N_KEYS - 1)
        gtt_scr[rows, :] = gate
        return carry

    lax.fori_loop(0, PEER_HEADS, head, 0)
    e1_ref[...] = e1t_scr[...].T
    e2_ref[...] = e2t_scr[...].T
    gt_ref[...] = gtt_scr[...].T


def _route(x, g, sc, sh, wq, keys, per_token, rows_per_seq):
    t = x.shape[0]
    tr = min(ROUTE_BLOCK, t)
    nslot = PEER_HEADS * PEER_TOPK
    mspec = _mod_spec(per_token, tr, rows_per_seq)
    out = pl.BlockSpec((tr, nslot), lambda i: (i, 0))
    kk = PEER_TOPK
    return pl.pallas_call(
        functools.partial(_route_kernel, tr=tr),
        grid=(t // tr,),
        in_specs=[pl.BlockSpec((tr, D_MODEL), lambda i: (i, 0)),
                  pl.BlockSpec((1, D_MODEL), lambda i: (0, 0)),
                  mspec, mspec,
                  pl.BlockSpec(wq.shape, lambda i: (0, 0)),
                  pl.BlockSpec(keys.shape, lambda i: (0, 0, 0))],
        out_specs=[out, out, out],
        out_shape=[jax.ShapeDtypeStruct((t, nslot), I32),
                   jax.ShapeDtypeStruct((t, nslot), I32),
                   jax.ShapeDtypeStruct((t, nslot), F32)],
        scratch_shapes=[pltpu.VMEM((2 * PEER_HEADS, tr, N_KEYS), BF16),
                        pltpu.VMEM((N_KEYS, tr), F32),
                        pltpu.VMEM((kk * kk, tr), F32),
                        pltpu.VMEM((kk * kk, tr), I32),
                        pltpu.VMEM((kk, tr), F32), pltpu.VMEM((kk, tr), I32),
                        pltpu.VMEM((kk, tr), F32), pltpu.VMEM((kk, tr), I32),
                        pltpu.VMEM((kk, tr), F32), pltpu.VMEM((kk, tr), I32),
                        pltpu.VMEM((nslot, tr), I32), pltpu.VMEM((nslot, tr), I32),
                        pltpu.VMEM((nslot, tr), F32)],
        compiler_params=_cparams(("parallel",)),
        name="peer_route",
    )(x, g.reshape(1, D_MODEL), sc, sh, wq, keys)


def _gelu(x):
    return 0.5 * x * (1.0 + lax.erf(x * (2.0 ** -0.5)))


def _experts_kernel(x_ref, g_ref, sc_ref, sh_ref, gg_ref, e1_ref, e2_ref, gt_ref, u_ref, v_ref,
                    o_ref, h_scr, w_scr, acc_scr, p_scr, *, ts):
    c = pl.program_id(1)
    nsl = N_KEYS // 2

    @pl.when(c == 0)
    def _():
        h_scr[...] = _norm_mod(x_ref[...], g_ref[...], sc_ref[...], sh_ref[...]).astype(BF16)
        rho = lax.broadcasted_iota(I32, (N_KEYS, N_KEYS), 0)
        pi = (rho & ~15) | ((rho & 7) << 1) | ((rho >> 3) & 1)

        def body(tb, carry):
            base = pl.multiple_of(tb * SUBLANES, SUBLANES)
            e1b = e1_ref[pl.ds(base, SUBLANES), :]
            e2b = e2_ref[pl.ds(base, SUBLANES), :]
            gtb = gt_ref[pl.ds(base, SUBLANES), :]
            for i in range(SUBLANES):
                r1 = jnp.where(pi == e1b[i:i + 1, :], 1.0, 0.0).astype(BF16)
                r2 = jnp.where(rho == e2b[i:i + 1, :], gtb[i:i + 1, :], 0.0).astype(BF16)
                wt = lax.dot_general(r1, r2, (((1,), (1,)), ((), ())), preferred_element_type=F32)
                lo = jnp.concatenate([wt[16 * m:16 * m + 8, :] for m in range(8)], axis=0)
                hi = jnp.concatenate([wt[16 * m + 8:16 * m + 16, :] for m in range(8)], axis=0)
                lo = pltpu.bitcast(lo.astype(BF16).astype(F32), U32) >> 16
                hi = pltpu.bitcast(hi.astype(BF16).astype(F32), U32)
                w_scr[pl.ds(base + i, nsl, stride=W_PITCH), :] = lo | hi
            return carry

        lax.fori_loop(0, ts // SUBLANES, body, 0)

    hmat = h_scr[...]
    for s in range(SUBLANES):
        a2 = lax.dot_general(hmat, u_ref[s * 256:(s + 1) * 256, :], (((1,), (1,)), ((), ())),
                             preferred_element_type=F32)
        start = pl.multiple_of((c * SUBLANES + s) * W_PITCH, SUBLANES)
        packed = w_scr[pl.ds(start, ts), :]
        lo = pltpu.bitcast(packed << 16, F32)
        hi = pltpu.bitcast(packed & jnp.uint32(0xFFFF0000), F32)
        w2 = jnp.concatenate([lo, hi], axis=1)
        p_scr[:, s * 256:(s + 1) * 256] = (_gelu(a2) * w2).astype(BF16)
    part = jnp.dot(p_scr[...], v_ref[...], preferred_element_type=F32)

    @pl.when(c == 0)
    def _():
        acc_scr[...] = part

    @pl.when(c > 0)
    def _():
        acc_scr[...] += part

    @pl.when(c == pl.num_programs(1) - 1)
    def _():
        o_ref[...] = x_ref[...] + gg_ref[...] * acc_scr[...]


def _experts(x, g, sc, sh, gg, e1, e2, gt, u, v, per_token, rows_per_seq):
    t = x.shape[0]
    ts = EXPERT_ROWS
    assert t % ts == 0
    nslot = PEER_HEADS * PEER_TOPK
    mspec = _mod_spec(per_token, ts, rows_per_seq)
    slot = pl.BlockSpec((ts, nslot), lambda i, c: (i, 0))
    tab = pl.BlockSpec((EXPERT_CHUNK, D_MODEL), lambda i, c: (c, 0))
    return pl.pallas_call(
        functools.partial(_experts_kernel, ts=ts),
        grid=(t // ts, N_KEYS * N_KEYS // EXPERT_CHUNK),
        in_specs=[pl.BlockSpec((ts, D_MODEL), lambda i, c: (i, 0)),
                  pl.BlockSpec((1, D_MODEL), lambda i, c: (0, 0)),
                  mspec, mspec, mspec, slot, slot, slot, tab, tab],
        out_specs=pl.BlockSpec((ts, D_MODEL), lambda i, c: (i, 0)),
        out_shape=jax.ShapeDtypeStruct((t, D_MODEL), F32),
        scratch_shapes=[pltpu.VMEM((ts, D_MODEL), BF16),
                        pltpu.VMEM((N_KEYS // 2 * W_PITCH, LANES), U32),
                        pltpu.VMEM((ts, D_MODEL), F32),
                        pltpu.VMEM((ts, EXPERT_CHUNK), BF16)],
        compiler_params=_cparams(("parallel", "arbitrary")),
        name="peer_experts",
    )(x, g.reshape(1, D_MODEL), sc, sh, gg, e1, e2, gt, u, v)


def _final_kernel(x_ref, g_ref, o_ref):
    x = x_ref[...]
    o_ref[...] = x * lax.rsqrt(jnp.mean(x * x, axis=-1, keepdims=True) + NORM_EPS) * g_ref[...]


def _final_norm(x, g):
    t = x.shape[0]
    tm = min(TOKEN_BLOCK, t)
    return pl.pallas_call(
        _final_kernel,
        grid=(t // tm,),
        in_specs=[pl.BlockSpec((tm, D_MODEL), lambda i: (i, 0)),
                  pl.BlockSpec((1, D_MODEL), lambda i: (0, 0))],
        out_specs=pl.BlockSpec((tm, D_MODEL), lambda i: (i, 0)),
        out_shape=jax.ShapeDtypeStruct((t, D_MODEL), F32),
        compiler_params=_cparams(("parallel",)),
        name="final_norm",
    )(x, g.reshape(1, D_MODEL))


def _pad_rows(a, n, length, rows):
    if rows == length:
        return a
    a = a.reshape(n, length, a.shape[-1])
    a = jnp.pad(a, ((0, 0), (0, rows - length), (0, 0)))
    return a.reshape(n * rows, a.shape[-1])


def _unpad_rows(a, n, length, rows):
    if rows == length:
        return a
    return a.reshape(n, rows, a.shape[-1])[:, :length].reshape(n * length, a.shape[-1])


def _trunk(x, mod, pos0, ret_s, conv_s, hg_s, w):
    n, length, _ = x.shape
    t = n * length
    per_token = length < TOKEN_BLOCK
    rows = max(length, SUBLANES)
    x = x.reshape(t, D_MODEL)

    pos = pos0 + jnp.arange(rows, dtype=F32)
    half = RET_DK // 2
    inv = 1.0 / (ROPE_BASE ** jnp.linspace(0.0, 1.0, half, dtype=F32))
    ang = pos[:, None] * inv[None, :]
    cos = jnp.concatenate([jnp.cos(ang), jnp.cos(ang)], axis=-1)
    sin = jnp.concatenate([-jnp.sin(ang), jnp.sin(ang)], axis=-1)
    log_gamma = jnp.log(1.0 - 2.0 ** (-5.0 - jnp.arange(RET_HEADS, dtype=F32)))
    sm = jax.nn.softmax(w['hg_lower_bounds'].astype(F32), axis=0)
    lbs = jnp.cumsum(sm, axis=0) - sm[0]

    def mods(l):
        parts = jnp.split(mod[l], 6, axis=-1)
        if per_token:
            return [jnp.repeat(p, length, axis=0) for p in parts]
        return [p.reshape(n, 1, D_MODEL) for p in parts]

    new_ret, new_conv, new_hg = [], [], []
    for l in range(DEPTH):
        sh1, sc1, g1, sh2, sc2, g2 = mods(l)
        j = l // 2
        if l % 2 == 0:
            proj = _norm_mod_matmul(x, w['norm_mix_g'][l], sc1, sh1, w['ab_w_in'][j], per_token, length)
            projp = _pad_rows(proj, n, length, rows)
            ret_out, s_r = _retention(projp, cos, sin, log_gamma, w['ret_norm_g'][j], ret_s[j],
                                      n, rows, length)
            zprev8 = jnp.pad(conv_s[j], ((0, 0), (SUBLANES - 2, 0), (0, 0)))
            conv_out, tail = _conv(projp, w['conv_w'][j], zprev8, n, rows)
            r0 = (length - 2) % SUBLANES if rows == length else length - 2
            new_ret.append(s_r)
            new_conv.append(tail[:, r0:r0 + 2, :])
            acts = [_unpad_rows(ret_out, n, length, rows), _unpad_rows(conv_out, n, length, rows)]
            wo = w['ab_w_out'][j]
            x = _outproj(x, g1, acts, [wo[:RET_WIDTH], wo[RET_WIDTH:]], per_token, length)
        else:
            proj = _norm_mod_matmul(x, w['norm_mix_g'][l], sc1, sh1, w['hg_w_in'][j], per_token, length)
            projp = _pad_rows(proj, n, length, rows)
            o, s_h = _hgrn(projp, lbs[l], w['hg_norm_g'][j], hg_s[j], n, rows, length)
            new_hg.append(s_h)
            x = _outproj(x, g1, [_unpad_rows(o, n, length, rows)], [w['hg_w_out'][j]], per_token, length)
        e1, e2, gt = _route(x, w['norm_ffn_g'][l], sc2, sh2, w['peer_wq'][l], w['peer_keys'][l],
                            per_token, length)
        x = _experts(x, w['norm_ffn_g'][l], sc2, sh2, g2, e1, e2, gt, w['peer_u'][l], w['peer_v'][l],
                     per_token, length)
    y = _final_norm(x, w['final_norm_g']).reshape(n, length, D_MODEL)
    return y, jnp.stack(new_ret), jnp.stack(new_conv), jnp.stack(new_hg)


def kernel(x_prompt, x_sample, state_ret, state_conv, state_hgrn, c_prompt, c_sample, ada_w, ada_b, norm_mix_g, norm_ffn_g, ab_w_in, ret_norm_g, conv_w, ab_w_out, hg_w_in, hg_lower_bounds, hg_norm_g, hg_w_out, peer_wq, peer_keys, peer_u, peer_v, final_norm_g):
    nb = x_prompt.shape[0]
    n_even, n_odd = state_ret.shape[0], state_hgrn.shape[0]
    w = dict(
        norm_mix_g=norm_mix_g, norm_ffn_g=norm_ffn_g, ret_norm_g=ret_norm_g, conv_w=conv_w,
        hg_lower_bounds=hg_lower_bounds, hg_norm_g=hg_norm_g, final_norm_g=final_norm_g,
        ab_w_in=ab_w_in.astype(BF16), ab_w_out=ab_w_out.astype(BF16),
        hg_w_in=hg_w_in.astype(BF16), hg_w_out=hg_w_out.astype(BF16),
        peer_wq=peer_wq.astype(BF16),
        peer_keys=peer_keys.astype(BF16).reshape(DEPTH, 2 * PEER_HEADS, N_KEYS, N_KEYS),
        peer_u=peer_u.astype(BF16), peer_v=peer_v.astype(BF16),
    )
    mod = _ada(jnp.concatenate([c_prompt, c_sample], axis=0), ada_w, ada_b)
    mod_p, mod_s = mod[:, :nb], mod[:, nb:]
    dt = x_prompt.dtype
    ret0 = jnp.zeros((n_even, nb) + state_ret.shape[2:], dt)
    conv0 = jnp.zeros((n_even, nb) + state_conv.shape[2:], dt)
    hg0 = jnp.zeros((n_odd, nb) + state_hgrn.shape[2:], dt)
    y_p, ret_p, conv_p, hg_p = _trunk(x_prompt, mod_p, 0.0, ret0, conv0, hg0, w)
    y_s, ret_s, conv_s, hg_s = _trunk(x_sample, mod_s, float(PAST_LEN), state_ret, state_conv, state_hgrn, w)
    return (y_p, y_s, ret_p, conv_p, hg_p, ret_s, conv_s, hg_s)
```

```python
import functools
import math

import jax
import jax.numpy as jnp
from jax import lax
from jax.experimental import pallas as pl
from jax.experimental.pallas import tpu as pltpu

F32 = jnp.float32
BF16 = jnp.bfloat16
I32 = jnp.int32
U32 = jnp.uint32

D_MODEL = 1024
DEPTH = 4
PAST_LEN = 16384
RET_HEADS = 4
RET_DK = 128
RET_WIDTH = RET_HEADS * RET_DK
RET_CHUNK = 128
ROPE_BASE = 10000.0
CONV_DIM = D_MODEL // 2
HG_HEADS = 8
HG_DK = 128
HG_SUB = 16
HG_BLOCK = 256
PEER_HEADS = 8
N_KEYS = 128
PEER_TOPK = 16
NORM_EPS = 1e-6

LANES = 128
SUBLANES = 8
VMEM_LIMIT = 56 * 1024 * 1024

TOKEN_BLOCK = 512
ROUTE_BLOCK = 256
ROUTE_HEADS = 2
EXPERT_ROWS = 512
EXPERT_CHUNK = 2048
W_PITCH = EXPERT_ROWS + SUBLANES


def _cparams(sem):
    return pltpu.CompilerParams(dimension_semantics=sem, vmem_limit_bytes=VMEM_LIMIT)


def _mod_spec(per_token, tm, rows_per_seq):
    if per_token:
        return pl.BlockSpec((tm, D_MODEL), lambda i, *_: (i, 0))
    return pl.BlockSpec((None, 1, D_MODEL), lambda i, *_: (i * tm // rows_per_seq, 0, 0))


def _norm_mod(x, g, sc, sh):
    y = x * lax.rsqrt(jnp.mean(x * x, axis=-1, keepdims=True) + NORM_EPS)
    return (y * g) * (1.0 + sc) + sh


def _silu(x):
    return x * (1.0 / (1.0 + jnp.exp(-x)))


def _sigmoid(x):
    return 1.0 / (1.0 + jnp.exp(-x))


def _ada_kernel(c_ref, w_ref, b_ref, o_ref):
    a = _silu(c_ref[...]).astype(BF16)
    o_ref[...] = jnp.dot(a, w_ref[...].astype(BF16), preferred_element_type=F32) + b_ref[...]


def _ada(c, ada_w, ada_b):
    n = c.shape[0]
    tn = 1536
    nj = 6 * D_MODEL // tn
    return pl.pallas_call(
        _ada_kernel,
        grid=(DEPTH, nj),
        in_specs=[
            pl.BlockSpec((n, D_MODEL), lambda l, j: (0, 0)),
            pl.BlockSpec((None, D_MODEL, tn), lambda l, j: (l, 0, j)),
            pl.BlockSpec((None, 1, tn), lambda l, j: (l, 0, j)),
        ],
        out_specs=pl.BlockSpec((None, n, tn), lambda l, j: (l, 0, j)),
        out_shape=jax.ShapeDtypeStruct((DEPTH, n, 6 * D_MODEL), F32),
        compiler_params=_cparams(("parallel", "parallel")),
        name="ada_mod",
    )(c, ada_w, ada_b.reshape(DEPTH, 1, 6 * D_MODEL))


def _nmm_kernel(x_ref, g_ref, sc_ref, sh_ref, w_ref, o_ref, h_ref):
    @pl.when(pl.program_id(1) == 0)
    def _():
        h_ref[...] = _norm_mod(x_ref[...], g_ref[...], sc_ref[...], sh_ref[...]).astype(BF16)

    o_ref[...] = jnp.dot(h_ref[...], w_ref[...], preferred_element_type=F32)


def _norm_mod_matmul(x, g, sc, sh, w, per_token, rows_per_seq):
    t = x.shape[0]
    f = w.shape[1]
    tm = min(TOKEN_BLOCK, t)
    tn = 512
    mspec = _mod_spec(per_token, tm, rows_per_seq)
    return pl.pallas_call(
        _nmm_kernel,
        grid=(t // tm, f // tn),
        in_specs=[
            pl.BlockSpec((tm, D_MODEL), lambda i, j: (i, 0)),
            pl.BlockSpec((1, D_MODEL), lambda i, j: (0, 0)),
            mspec, mspec,
            pl.BlockSpec((D_MODEL, tn), lambda i, j: (0, j)),
        ],
        out_specs=pl.BlockSpec((tm, tn), lambda i, j: (i, j)),
        out_shape=jax.ShapeDtypeStruct((t, f), F32),
        scratch_shapes=[pltpu.VMEM((tm, D_MODEL), BF16)],
        compiler_params=_cparams(("parallel", "arbitrary")),
        name="norm_mod_proj",
    )(x, g.reshape(1, D_MODEL), sc, sh, w)


def _ret_kernel(lg_ref, q_ref, k_ref, v_ref, g_ref, cos_ref, sin_ref, rg_ref, s0_ref,
                o_ref, sout_ref, s_scr, *, chunk, real):
    c = pl.program_id(1)

    @pl.when(c == 0)
    def _():
        s_scr[...] = s0_ref[...]

    cos = cos_ref[...]
    sin = sin_ref[...]

    def rope(x):
        return x * cos + pltpu.roll(x, RET_DK // 2, 1) * sin

    ri = lax.broadcasted_iota(I32, (chunk, chunk), 0)
    ci = lax.broadcasted_iota(I32, (chunk, chunk), 1)
    causal = ri >= ci
    rel = jnp.where(causal, ri - ci, 0).astype(F32)
    pos = lax.broadcasted_iota(I32, (chunk, RET_DK), 0).astype(F32)

    for hh in range(RET_HEADS):
        cols = slice(hh * RET_DK, (hh + 1) * RET_DK)
        lg = lg_ref[hh]
        decay = jnp.where(causal, jnp.exp(lg * rel), 0.0)
        in_decay = jnp.exp(lg * (pos + 1.0))
        st_decay = jnp.exp(lg * (real - 1.0 - pos))
        ch_decay = jnp.exp(jnp.full((1, RET_DK), real, F32) * lg)
        q = rope(q_ref[:, cols])
        k = rope(k_ref[:, cols]) * (RET_DK ** -0.5)
        v = v_ref[:, cols]
        s = s_scr[hh]
        att = lax.dot_general(q, k, (((1,), (1,)), ((), ())), preferred_element_type=F32) * decay
        o = (jnp.dot(att, v, preferred_element_type=F32)
             + jnp.dot(q * in_decay, s, preferred_element_type=F32))
        s_scr[hh] = s * ch_decay + lax.dot_general(k * st_decay, v, (((0,), (0,)), ((), ())),
                                                   preferred_element_type=F32)
        y = o * lax.rsqrt(jnp.mean(o * o, axis=-1, keepdims=True) + NORM_EPS) * rg_ref[:, cols]
        o_ref[:, cols] = _silu(g_ref[:, cols]) * y

    @pl.when(c == pl.num_programs(1) - 1)
    def _():
        sout_ref[...] = s_scr[...]


def _retention(proj, cos, sin, log_gamma, ret_g, s0, n, rows, real):
    chunk = min(rows, RET_CHUNK)
    nc = rows // chunk
    h = RET_HEADS

    def col(j):
        return pl.BlockSpec((chunk, RET_WIDTH), lambda b, c, *_: (b * nc + c, j))

    tab = pl.BlockSpec((chunk, RET_DK), lambda b, c, *_: (c, 0))
    st = pl.BlockSpec((None, h, RET_DK, RET_DK), lambda b, c, *_: (b, 0, 0, 0))
    grid_spec = pltpu.PrefetchScalarGridSpec(
        num_scalar_prefetch=1,
        grid=(n, nc),
        in_specs=[col(0), col(1), col(2), col(3), tab, tab,
                  pl.BlockSpec((1, RET_WIDTH), lambda b, c, *_: (0, 0)), st],
        out_specs=[col(0), st],
        scratch_shapes=[pltpu.VMEM((h, RET_DK, RET_DK), F32)],
    )
    return pl.pallas_call(
        functools.partial(_ret_kernel, chunk=chunk, real=float(real if nc == 1 else chunk)),
        grid_spec=grid_spec,
        out_shape=[jax.ShapeDtypeStruct((n * rows, RET_WIDTH), F32),
                   jax.ShapeDtypeStruct((n, h, RET_DK, RET_DK), F32)],
        compiler_params=_cparams(("parallel", "arbitrary")),
        name="retention",
    )(log_gamma, proj, proj, proj, proj, cos, sin, ret_g.reshape(1, RET_WIDTH), s0)


def _conv_kernel(bg_ref, cg_ref, xi_ref, w_ref, zp_ref, o_ref, tail_ref, prev_scr, *, chunk):
    c = pl.program_id(1)

    @pl.when(c == 0)
    def _():
        prev_scr[...] = zp_ref[...]

    z = cg_ref[...] * xi_ref[...]
    p2 = prev_scr[SUBLANES - 2:SUBLANES - 1, :]
    p1 = prev_scr[SUBLANES - 1:SUBLANES, :]
    row = lax.broadcasted_iota(I32, (chunk, CONV_DIM), 0)
    z1 = jnp.where(row == 0, p1, pltpu.roll(z, 1, 0))
    z2 = jnp.where(row == 0, p2, jnp.where(row == 1, p1, pltpu.roll(z, 2, 0)))
    conv = w_ref[0:1, :] * z2 + w_ref[1:2, :] * z1 + w_ref[2:3, :] * z
    o_ref[...] = bg_ref[...] * conv
    tail = z[chunk - SUBLANES:, :]
    prev_scr[...] = tail
    tail_ref[...] = tail


def _conv(proj, conv_w, zprev8, n, rows):
    chunk = min(rows, TOKEN_BLOCK)
    nc = rows // chunk
    base = 4 * RET_WIDTH // CONV_DIM

    def col(j):
        return pl.BlockSpec((chunk, CONV_DIM), lambda b, c: (b * nc + c, base + j))

    return pl.pallas_call(
        functools.partial(_conv_kernel, chunk=chunk),
        grid=(n, nc),
        in_specs=[col(0), col(1), col(2),
                  pl.BlockSpec((3, CONV_DIM), lambda b, c: (0, 0)),
                  pl.BlockSpec((None, SUBLANES, CONV_DIM), lambda b, c: (b, 0, 0))],
        out_specs=[pl.BlockSpec((chunk, CONV_DIM), lambda b, c: (b * nc + c, 0)),
                   pl.BlockSpec((None, SUBLANES, CONV_DIM), lambda b, c: (b, 0, 0))],
        out_shape=[jax.ShapeDtypeStruct((n * rows, CONV_DIM), F32),
                   jax.ShapeDtypeStruct((n, SUBLANES, CONV_DIM), F32)],
        scratch_shapes=[pltpu.VMEM((SUBLANES, CONV_DIM), F32)],
        compiler_params=_cparams(("parallel", "arbitrary")),
        name="short_conv",
    )(proj, proj, proj, conv_w, zprev8)


def _hgrn_kernel(q_ref, f_ref, i_ref, g_ref, lb_ref, ng_ref, s0_ref, o_ref, sout_ref,
                 st_scr, b_scr, q_scr, k_scr, o_scr, *, block, sub, real):
    c = pl.program_id(1)

    @pl.when(c == 0)
    def _():
        for hh in range(HG_HEADS):
            st_scr[hh] = s0_ref[hh].T

    lb = lb_ref[...]
    fg = lb + (1.0 - lb) * _sigmoid(f_ref[...])
    k_scr[...] = 1.0 - fg
    q_scr[...] = _silu(q_ref[...])
    b = jnp.log(fg)
    rowm = lax.broadcasted_iota(I32, b.shape, 0) & (sub - 1)
    step = 1
    while step < sub:
        b = b + jnp.where(rowm >= step, pltpu.roll(b, step, 0), 0.0)
        step *= 2
    b_scr[...] = b

    ones = jnp.ones((HG_DK, HG_DK), BF16)
    rowi = lax.broadcasted_iota(I32, (sub, HG_DK), 0)
    nsub = block // sub
    last = (real - 1) if nsub == 1 else (sub - 1)

    def body(sb, carry):
        rows = pl.ds(pl.multiple_of(sb * sub, sub), sub)
        for hh in range(HG_HEADS):
            cols = slice(hh * HG_DK, (hh + 1) * HG_DK)
            b = b_scr[rows, cols]
            q = q_scr[rows, cols]
            kk = k_scr[rows, cols]
            iv = i_ref[rows, cols]
            st = st_scr[hh]
            o = lax.dot_general((q * jnp.exp(b)).astype(BF16), st.astype(BF16),
                                (((1,), (1,)), ((), ())), preferred_element_type=F32)
            parts = []
            for j in range(sub):
                e = jnp.exp(jnp.where(rowi >= j, b - b[j:j + 1, :], -jnp.inf))
                parts.append(q * kk[j:j + 1, :] * e)
            t_all = jnp.concatenate(parts, axis=0).astype(BF16)
            s_all = jnp.dot(t_all, ones, preferred_element_type=F32)
            for j in range(sub):
                o = o + s_all[j * sub:(j + 1) * sub, :] * iv[j:j + 1, :]
            b_end = b[last:last + 1, :]
            kd = (kk * jnp.exp(b_end - b)).astype(BF16)
            st_scr[hh] = st * jnp.exp(b_end) + lax.dot_general(
                iv.astype(BF16), kd, (((0,), (0,)), ((), ())), preferred_element_type=F32)
            o_scr[rows, cols] = o
        return carry

    lax.fori_loop(0, nsub, body, 0)

    for hh in range(HG_HEADS):
        cols = slice(hh * HG_DK, (hh + 1) * HG_DK)
        o = o_scr[:, cols]
        y = o * lax.rsqrt(jnp.mean(o * o, axis=-1, keepdims=True) + NORM_EPS) * ng_ref[:, cols]
        o_ref[:, cols] = y * _silu(g_ref[:, cols])

    @pl.when(c == pl.num_programs(1) - 1)
    def _():
        for hh in range(HG_HEADS):
            sout_ref[hh] = st_scr[hh].T


def _hgrn(proj, lb, norm_g, s0, n, rows, real):
    block = min(rows, HG_BLOCK)
    nc = rows // block
    sub = min(HG_SUB, block)
    h = HG_HEADS
    width = h * HG_DK

    def col(j):
        return pl.BlockSpec((block, width), lambda b, c: (b * nc + c, j))

    vec = pl.BlockSpec((1, width), lambda b, c: (0, 0))
    st = pl.BlockSpec((None, h, HG_DK, HG_DK), lambda b, c: (b, 0, 0, 0))
    return pl.pallas_call(
        functools.partial(_hgrn_kernel, block=block, sub=sub, real=real),
        grid=(n, nc),
        in_specs=[col(0), col(1), col(2), col(3), vec, vec, st],
        out_specs=[col(0), st],
        out_shape=[jax.ShapeDtypeStruct((n * rows, width), F32),
                   jax.ShapeDtypeStruct((n, h, HG_DK, HG_DK), F32)],
        scratch_shapes=[pltpu.VMEM((h, HG_DK, HG_DK), F32)] + [pltpu.VMEM((block, width), F32)] * 4,
        compiler_params=_cparams(("parallel", "arbitrary")),
        name="hgrn2",
    )(proj, proj, proj, proj, lb.reshape(1, width), norm_g.reshape(1, width), s0)


def _outproj_kernel(*refs, n_in):
    x_ref, gate_ref = refs[0], refs[1]
    a_refs = refs[2:2 + n_in]
    w_refs = refs[2 + n_in:2 + 2 * n_in]
    o_ref = refs[2 + 2 * n_in]
    y = jnp.dot(a_refs[0][...].astype(BF16), w_refs[0][...], preferred_element_type=F32)
    for a_ref, w_ref in zip(a_refs[1:], w_refs[1:]):
        y = y + jnp.dot(a_ref[...].astype(BF16), w_ref[...], preferred_element_type=F32)
    o_ref[...] = x_ref[...] + gate_ref[...] * y


def _outproj(x, gate, acts, ws, per_token, rows_per_seq):
    t = x.shape[0]
    tm = min(TOKEN_BLOCK, t)
    n_in = len(acts)
    in_specs = [pl.BlockSpec((tm, D_MODEL), lambda i: (i, 0)), _mod_spec(per_token, tm, rows_per_seq)]
    in_specs += [pl.BlockSpec((tm, a.shape[1]), lambda i: (i, 0)) for a in acts]
    in_specs += [pl.BlockSpec(w.shape, lambda i: (0, 0)) for w in ws]
    return pl.pallas_call(
        functools.partial(_outproj_kernel, n_in=n_in),
        grid=(t // tm,),
        in_specs=in_specs,
        out_specs=pl.BlockSpec((tm, D_MODEL), lambda i: (i, 0)),
        out_shape=jax.ShapeDtypeStruct((t, D_MODEL), F32),
        compiler_params=_cparams(("parallel",)),
        name="out_proj",
    )(x, gate, *acts, *ws)


def _load(ref):
    return ref[...]


def _topk16(problems):
    def body(k, carry):
        for v_ref, code_fn, val_out, code_out in problems:
            v = v_ref[...]
            code = code_fn()
            m = jnp.max(v, axis=0, keepdims=True)
            cmin = jnp.min(jnp.where(v == m, code, jnp.inf), axis=0, keepdims=True)
            v_ref[...] = jnp.where(code == cmin, -jnp.inf, v)
            val_out[pl.ds(k, 1), :] = m
            code_out[pl.ds(k, 1), :] = cmin
        return carry

    lax.fori_loop(0, PEER_TOPK, body, 0)


def _route_kernel(x_ref, g_ref, sc_ref, sh_ref, wq_ref, keys_ref, e1_ref, e2_ref, gt_ref,
                  q_scr, s_scr, comb_scr, code_scr, v_scr, i_scr,
                  vs_scr, cs_scr, e1t_scr, e2t_scr, gtt_scr, iota_scr, *, tr):
    h = _norm_mod(x_ref[...], g_ref[...], sc_ref[...], sh_ref[...]).astype(BF16)
    q = jnp.dot(h, wq_ref[...], preferred_element_type=F32).astype(BF16)
    for hp in range(2 * PEER_HEADS):
        q_scr[hp] = q[:, hp * N_KEYS:(hp + 1) * N_KEYS]

    kk = PEER_TOPK
    nexp = float(N_KEYS * N_KEYS)
    iota_scr[...] = lax.broadcasted_iota(I32, (N_KEYS, tr), 0).astype(F32)
    key_iota = functools.partial(_load, iota_scr)

    def heads(it, carry):
        first = []
        for u in range(2 * ROUTE_HEADS):
            hp = 2 * ROUTE_HEADS * it + u
            s_scr[u] = lax.dot_general(keys_ref[hp], q_scr[hp], (((1,), (1,)), ((), ())),
                                       preferred_element_type=F32)
            first.append((s_scr.at[u], key_iota, v_scr.at[u], i_scr.at[u]))
        _topk16(first)
        second = []
        for u in range(ROUTE_HEADS):
            v1_ref, i1_ref = v_scr.at[2 * u], i_scr.at[2 * u]
            v2 = v_scr[2 * u + 1]
            i2 = i_scr[2 * u + 1]
            comb, code = comb_scr.at[u], code_scr.at[u]
            off = 0
            for a in range(kk // 2):
                nb = kk // (a + 1)
                rows = -(-nb // SUBLANES) * SUBLANES
                bpos = lax.broadcasted_iota(I32, (rows, tr), 0)
                val = v1_ref[a:a + 1, :] + v2[0:rows, :]
                if nb < rows:
                    val = jnp.where(bpos < nb, val, -jnp.inf)
                comb[off:off + rows, :] = val
                code[off:off + rows, :] = ((bpos + a * kk).astype(F32) * nexp
                                           + (i1_ref[a:a + 1, :] * N_KEYS + i2[0:rows, :]))
                off += rows
            apos = lax.broadcasted_iota(I32, (kk // 2, tr), 0) + kk // 2
            comb[off:off + kk // 2, :] = v1_ref[kk // 2:kk, :] + v2[0:1, :]
            code[off:off + kk // 2, :] = ((apos * kk).astype(F32) * nexp
                                          + (i1_ref[kk // 2:kk, :] * N_KEYS + i2[0:1, :]))
            second.append((comb, functools.partial(_load, code), vs_scr.at[u], cs_scr.at[u]))
        _topk16(second)
        for u in range(ROUTE_HEADS):
            vals = vs_scr[u]
            ex = jnp.exp(vals - vals[0:1, :])
            gate = ex / jnp.sum(ex, axis=0, keepdims=True)
            eid = cs_scr[u].astype(I32) & (N_KEYS * N_KEYS - 1)
            rows = pl.ds(pl.multiple_of((ROUTE_HEADS * it + u) * kk, kk), kk)
            e1t_scr[rows, :] = eid >> 7
            e2t_scr[rows, :] = eid & (N_KEYS - 1)
            gtt_scr[rows, :] = gate
        return carry

    lax.fori_loop(0, PEER_HEADS // ROUTE_HEADS, heads, 0)
    e1_ref[...] = e1t_scr[...].T
    e2_ref[...] = e2t_scr[...].T
    gt_ref[...] = gtt_scr[...].T


def _route(x, g, sc, sh, wq, keys, per_token, rows_per_seq):
    t = x.shape[0]
    tr = min(ROUTE_BLOCK, t)
    nslot = PEER_HEADS * PEER_TOPK
    mspec = _mod_spec(per_token, tr, rows_per_seq)
    out = pl.BlockSpec((tr, nslot), lambda i: (i, 0))
    kk = PEER_TOPK
    rh = ROUTE_HEADS
    ncand = sum(-(-(kk // (a + 1)) // SUBLANES) * SUBLANES for a in range(kk // 2)) + kk // 2
    return pl.pallas_call(
        functools.partial(_route_kernel, tr=tr),
        grid=(t // tr,),
        in_specs=[pl.BlockSpec((tr, D_MODEL), lambda i: (i, 0)),
                  pl.BlockSpec((1, D_MODEL), lambda i: (0, 0)),
                  mspec, mspec,
                  pl.BlockSpec(wq.shape, lambda i: (0, 0)),
                  pl.BlockSpec(keys.shape, lambda i: (0, 0, 0))],
        out_specs=[out, out, out],
        out_shape=[jax.ShapeDtypeStruct((t, nslot), I32),
                   jax.ShapeDtypeStruct((t, nslot), I32),
                   jax.ShapeDtypeStruct((t, nslot), F32)],
        scratch_shapes=[pltpu.VMEM((2 * PEER_HEADS, tr, N_KEYS), BF16),
                        pltpu.VMEM((2 * rh, N_KEYS, tr), F32),
                        pltpu.VMEM((rh, ncand, tr), F32),
                        pltpu.VMEM((rh, ncand, tr), F32),
                        pltpu.VMEM((2 * rh, kk, tr), F32), pltpu.VMEM((2 * rh, kk, tr), F32),
                        pltpu.VMEM((rh, kk, tr), F32), pltpu.VMEM((rh, kk, tr), F32),
                        pltpu.VMEM((nslot, tr), I32), pltpu.VMEM((nslot, tr), I32),
                        pltpu.VMEM((nslot, tr), F32), pltpu.VMEM((N_KEYS, tr), F32)],
        compiler_params=_cparams(("parallel",)),
        name="peer_route",
    )(x, g.reshape(1, D_MODEL), sc, sh, wq, keys)


def _gelu(x):
    return 0.5 * x * (1.0 + lax.erf(x * (2.0 ** -0.5)))


def _experts_kernel(x_ref, g_ref, sc_ref, sh_ref, gg_ref, e1_ref, e2_ref, gt_ref, u_ref, v_ref,
                    o_ref, h_scr, w_scr, acc_scr, p_scr, *, ts):
    c = pl.program_id(1)
    nsl = N_KEYS // 2

    @pl.when(c == 0)
    def _():
        h_scr[...] = _norm_mod(x_ref[...], g_ref[...], sc_ref[...], sh_ref[...]).astype(BF16)
        rho = lax.broadcasted_iota(I32, (N_KEYS, N_KEYS), 0)
        pi = (rho & ~15) | ((rho & 7) << 1) | ((rho >> 3) & 1)

        def body(tb, carry):
            base = pl.multiple_of(tb * SUBLANES, SUBLANES)
            e1b = e1_ref[pl.ds(base, SUBLANES), :]
            e2b = e2_ref[pl.ds(base, SUBLANES), :]
            gtb = gt_ref[pl.ds(base, SUBLANES), :]
            for i in range(SUBLANES):
                r1 = jnp.where(pi == e1b[i:i + 1, :], 1.0, 0.0).astype(BF16)
                r2 = jnp.where(rho == e2b[i:i + 1, :], gtb[i:i + 1, :], 0.0).astype(BF16)
                wt = lax.dot_general(r1, r2, (((1,), (1,)), ((), ())), preferred_element_type=F32)
                lo = jnp.concatenate([wt[16 * m:16 * m + 8, :] for m in range(8)], axis=0)
                hi = jnp.concatenate([wt[16 * m + 8:16 * m + 16, :] for m in range(8)], axis=0)
                lo = pltpu.bitcast(lo.astype(BF16).astype(F32), U32) >> 16
                hi = pltpu.bitcast(hi.astype(BF16).astype(F32), U32)
                w_scr[pl.ds(base + i, nsl, stride=W_PITCH), :] = lo | hi
            return carry

        lax.fori_loop(0, ts // SUBLANES, body, 0, unroll=4)

    hmat = h_scr[...]
    for s in range(SUBLANES):
        a2 = lax.dot_general(hmat, u_ref[s * 256:(s + 1) * 256, :], (((1,), (1,)), ((), ())),
                             preferred_element_type=F32)
        start = pl.multiple_of((c * SUBLANES + s) * W_PITCH, SUBLANES)
        packed = w_scr[pl.ds(start, ts), :]
        lo = pltpu.bitcast(packed << 16, F32)
        hi = pltpu.bitcast(packed & jnp.uint32(0xFFFF0000), F32)
        w2 = jnp.concatenate([lo, hi], axis=1)
        p_scr[:, s * 256:(s + 1) * 256] = (_gelu(a2) * w2).astype(BF16)
    part = jnp.dot(p_scr[...], v_ref[...], preferred_element_type=F32)

    @pl.when(c == 0)
    def _():
        acc_scr[...] = part

    @pl.when(c > 0)
    def _():
        acc_scr[...] += part

    @pl.when(c == pl.num_programs(1) - 1)
    def _():
        o_ref[...] = x_ref[...] + gg_ref[...] * acc_scr[...]


def _experts(x, g, sc, sh, gg, e1, e2, gt, u, v, per_token, rows_per_seq):
    t = x.shape[0]
    ts = EXPERT_ROWS
    assert t % ts == 0
    nslot = PEER_HEADS * PEER_TOPK
    mspec = _mod_spec(per_token, ts, rows_per_seq)
    slot = pl.BlockSpec((ts, nslot), lambda i, c: (i, 0))
    tab = pl.BlockSpec((EXPERT_CHUNK, D_MODEL), lambda i, c: (c, 0))
    return pl.pallas_call(
        functools.partial(_experts_kernel, ts=ts),
        grid=(t // ts, N_KEYS * N_KEYS // EXPERT_CHUNK),
        in_specs=[pl.BlockSpec((ts, D_MODEL), lambda i, c: (i, 0)),
                  pl.BlockSpec((1, D_MODEL), lambda i, c: (0, 0)),
                  mspec, mspec, mspec, slot, slot, slot, tab, tab],
        out_specs=pl.BlockSpec((ts, D_MODEL), lambda i, c: (i, 0)),
        out_shape=jax.ShapeDtypeStruct((t, D_MODEL), F32),
        scratch_shapes=[pltpu.VMEM((ts, D_MODEL), BF16),
                        pltpu.VMEM((N_KEYS // 2 * W_PITCH, LANES), U32),
                        pltpu.VMEM((ts, D_MODEL), F32),
                        pltpu.VMEM((ts, EXPERT_CHUNK), BF16)],
        compiler_params=_cparams(("parallel", "arbitrary")),
        name="peer_experts",
    )(x, g.reshape(1, D_MODEL), sc, sh, gg, e1, e2, gt, u, v)


def _final_kernel(x_ref, g_ref, o_ref):
    x = x_ref[...]
    o_ref[...] = x * lax.rsqrt(jnp.mean(x * x, axis=-1, keepdims=True) + NORM_EPS) * g_ref[...]


def _final_norm(x, g):
    t = x.shape[0]
    tm = min(TOKEN_BLOCK, t)
    return pl.pallas_call(
        _final_kernel,
        grid=(t // tm,),
        in_specs=[pl.BlockSpec((tm, D_MODEL), lambda i: (i, 0)),
                  pl.BlockSpec((1, D_MODEL), lambda i: (0, 0))],
        out_specs=pl.BlockSpec((tm, D_MODEL), lambda i: (i, 0)),
        out_shape=jax.ShapeDtypeStruct((t, D_MODEL), F32),
        compiler_params=_cparams(("parallel",)),
        name="final_norm",
    )(x, g.reshape(1, D_MODEL))


def _pad_rows(a, n, length, rows):
    if rows == length:
        return a
    a = a.reshape(n, length, a.shape[-1])
    a = jnp.pad(a, ((0, 0), (0, rows - length), (0, 0)))
    return a.reshape(n * rows, a.shape[-1])


def _unpad_rows(a, n, length, rows):
    if rows == length:
        return a
    return a.reshape(n, rows, a.shape[-1])[:, :length].reshape(n * length, a.shape[-1])


def _trunk(x, mod, pos0, ret_s, conv_s, hg_s, w):
    n, length, _ = x.shape
    t = n * length
    per_token = length < TOKEN_BLOCK
    rows = max(length, SUBLANES)
    x = x.reshape(t, D_MODEL)

    pos = pos0 + jnp.arange(rows, dtype=F32)
    half = RET_DK // 2
    inv = 1.0 / (ROPE_BASE ** jnp.linspace(0.0, 1.0, half, dtype=F32))
    ang = pos[:, None] * inv[None, :]
    cos = jnp.concatenate([jnp.cos(ang), jnp.cos(ang)], axis=-1)
    sin = jnp.concatenate([-jnp.sin(ang), jnp.sin(ang)], axis=-1)
    log_gamma = jnp.log(1.0 - 2.0 ** (-5.0 - jnp.arange(RET_HEADS, dtype=F32)))
    sm = jax.nn.softmax(w['hg_lower_bounds'].astype(F32), axis=0)
    lbs = jnp.cumsum(sm, axis=0) - sm[0]

    def mods(l):
        parts = jnp.split(mod[l], 6, axis=-1)
        if per_token:
            return [jnp.repeat(p, length, axis=0) for p in parts]
        return [p.reshape(n, 1, D_MODEL) for p in parts]

    new_ret, new_conv, new_hg = [], [], []
    for l in range(DEPTH):
        sh1, sc1, g1, sh2, sc2, g2 = mods(l)
        j = l // 2
        if l % 2 == 0:
            proj = _norm_mod_matmul(x, w['norm_mix_g'][l], sc1, sh1, w['ab_w_in'][j], per_token, length)
            projp = _pad_rows(proj, n, length, rows)
            ret_out, s_r = _retention(projp, cos, sin, log_gamma, w['ret_norm_g'][j], ret_s[j],
                                      n, rows, length)
            zprev8 = jnp.pad(conv_s[j], ((0, 0), (SUBLANES - 2, 0), (0, 0)))
            conv_out, tail = _conv(projp, w['conv_w'][j], zprev8, n, rows)
            r0 = (length - 2) % SUBLANES if rows == length else length - 2
            new_ret.append(s_r)
            new_conv.append(tail[:, r0:r0 + 2, :])
            acts = [_unpad_rows(ret_out, n, length, rows), _unpad_rows(conv_out, n, length, rows)]
            wo = w['ab_w_out'][j]
            x = _outproj(x, g1, acts, [wo[:RET_WIDTH], wo[RET_WIDTH:]], per_token, length)
        else:
            proj = _norm_mod_matmul(x, w['norm_mix_g'][l], sc1, sh1, w['hg_w_in'][j], per_token, length)
            projp = _pad_rows(proj, n, length, rows)
            o, s_h = _hgrn(projp, lbs[l], w['hg_norm_g'][j], hg_s[j], n, rows, length)
            new_hg.append(s_h)
            x = _outproj(x, g1, [_unpad_rows(o, n, length, rows)], [w['hg_w_out'][j]], per_token, length)
        e1, e2, gt = _route(x, w['norm_ffn_g'][l], sc2, sh2, w['peer_wq'][l], w['peer_keys'][l],
                            per_token, length)
        x = _experts(x, w['norm_ffn_g'][l], sc2, sh2, g2, e1, e2, gt, w['peer_u'][l], w['peer_v'][l],
                     per_token, length)
    y = _final_norm(x, w['final_norm_g']).reshape(n, length, D_MODEL)
    return y, jnp.stack(new_ret), jnp.stack(new_conv), jnp.stack(new_hg)


def kernel(x_prompt, x_sample, state_ret, state_conv, state_hgrn, c_prompt, c_sample, ada_w, ada_b, norm_mix_g, norm_ffn_g, ab_w_in, ret_norm_g, conv_w, ab_w_out, hg_w_in, hg_lower_bounds, hg_norm_g, hg_w_out, peer_wq, peer_keys, peer_u, peer_v, final_norm_g):
    nb = x_prompt.shape[0]
    n_even, n_odd = state_ret.shape[0], state_hgrn.shape[0]
    w = dict(
        norm_mix_g=norm_mix_g, norm_ffn_g=norm_ffn_g, ret_norm_g=ret_norm_g, conv_w=conv_w,
        hg_lower_bounds=hg_lower_bounds, hg_norm_g=hg_norm_g, final_norm_g=final_norm_g,
        ab_w_in=ab_w_in.astype(BF16), ab_w_out=ab_w_out.astype(BF16),
        hg_w_in=hg_w_in.astype(BF16), hg_w_out=hg_w_out.astype(BF16),
        peer_wq=peer_wq.astype(BF16),
        peer_keys=peer_keys.astype(BF16).reshape(DEPTH, 2 * PEER_HEADS, N_KEYS, N_KEYS),
        peer_u=peer_u.astype(BF16), peer_v=peer_v.astype(BF16),
    )
    mod = _ada(jnp.concatenate([c_prompt, c_sample], axis=0), ada_w, ada_b)
    mod_p, mod_s = mod[:, :nb], mod[:, nb:]
    dt = x_prompt.dtype
    ret0 = jnp.zeros((n_even, nb) + state_ret.shape[2:], dt)
    conv0 = jnp.zeros((n_even, nb) + state_conv.shape[2:], dt)
    hg0 = jnp.zeros((n_odd, nb) + state_hgrn.shape[2:], dt)
    y_p, ret_p, conv_p, hg_p = _trunk(x_prompt, mod_p, 0.0, ret0, conv0, hg0, w)
    y_s, ret_s, conv_s, hg_s = _trunk(x_sample, mod_s, float(PAST_LEN), state_ret, state_conv, state_hgrn, w)
    return (y_p, y_s, ret_p, conv_p, hg_p, ret_s, conv_s, hg_s)
```

```python
import functools
import math

import jax
import jax.numpy as jnp
from jax import lax
from jax.experimental import pallas as pl
from jax.experimental.pallas import tpu as pltpu

F32 = jnp.float32
BF16 = jnp.bfloat16
I32 = jnp.int32
U32 = jnp.uint32

D_MODEL = 1024
DEPTH = 4
PAST_LEN = 16384
RET_HEADS = 4
RET_DK = 128
RET_WIDTH = RET_HEADS * RET_DK
RET_CHUNK = 128
ROPE_BASE = 10000.0
CONV_DIM = D_MODEL // 2
HG_HEADS = 8
HG_DK = 128
HG_SUB = 16
HG_BLOCK = 256
PEER_HEADS = 8
N_KEYS = 128
PEER_TOPK = 16
NORM_EPS = 1e-6

LANES = 128
SUBLANES = 8
VMEM_LIMIT = 56 * 1024 * 1024

TOKEN_BLOCK = 512
ROUTE_BLOCK = 256
ROUTE_HEADS = 2
EXPERT_ROWS = 512
EXPERT_CHUNK = 2048
W_PITCH = EXPERT_ROWS + SUBLANES


def _cparams(sem):
    return pltpu.CompilerParams(dimension_semantics=sem, vmem_limit_bytes=VMEM_LIMIT)


def _mod_spec(per_token, tm, rows_per_seq):
    if per_token:
        return pl.BlockSpec((tm, D_MODEL), lambda i, *_: (i, 0))
    return pl.BlockSpec((None, 1, D_MODEL), lambda i, *_: (i * tm // rows_per_seq, 0, 0))


def _norm_mod(x, g, sc, sh):
    y = x * lax.rsqrt(jnp.mean(x * x, axis=-1, keepdims=True) + NORM_EPS)
    return (y * g) * (1.0 + sc) + sh


def _silu(x):
    return x * (1.0 / (1.0 + jnp.exp(-x)))


def _sigmoid(x):
    return 1.0 / (1.0 + jnp.exp(-x))


def _ada_kernel(c_ref, w_ref, b_ref, o_ref):
    a = _silu(c_ref[...]).astype(BF16)
    o_ref[...] = jnp.dot(a, w_ref[...].astype(BF16), preferred_element_type=F32) + b_ref[...]


def _ada(c, ada_w, ada_b):
    n = c.shape[0]
    tn = 1536
    nj = 6 * D_MODEL // tn
    return pl.pallas_call(
        _ada_kernel,
        grid=(DEPTH, nj),
        in_specs=[
            pl.BlockSpec((n, D_MODEL), lambda l, j: (0, 0)),
            pl.BlockSpec((None, D_MODEL, tn), lambda l, j: (l, 0, j)),
            pl.BlockSpec((None, 1, tn), lambda l, j: (l, 0, j)),
        ],
        out_specs=pl.BlockSpec((None, n, tn), lambda l, j: (l, 0, j)),
        out_shape=jax.ShapeDtypeStruct((DEPTH, n, 6 * D_MODEL), F32),
        compiler_params=_cparams(("parallel", "parallel")),
        name="ada_mod",
    )(c, ada_w, ada_b.reshape(DEPTH, 1, 6 * D_MODEL))


def _nmm_kernel(x_ref, g_ref, sc_ref, sh_ref, w_ref, o_ref, *, tn):
    h = _norm_mod(x_ref[...], g_ref[...], sc_ref[...], sh_ref[...]).astype(BF16)
    for j in range(w_ref.shape[1] // tn):
        cols = slice(j * tn, (j + 1) * tn)
        o_ref[:, cols] = jnp.dot(h, w_ref[:, cols], preferred_element_type=F32)


def _norm_mod_matmul(x, g, sc, sh, w, per_token, rows_per_seq):
    t = x.shape[0]
    f = w.shape[1]
    tm = min(TOKEN_BLOCK, t)
    mspec = _mod_spec(per_token, tm, rows_per_seq)
    return pl.pallas_call(
        functools.partial(_nmm_kernel, tn=512),
        grid=(t // tm,),
        in_specs=[
            pl.BlockSpec((tm, D_MODEL), lambda i: (i, 0)),
            pl.BlockSpec((1, D_MODEL), lambda i: (0, 0)),
            mspec, mspec,
            pl.BlockSpec((D_MODEL, f), lambda i: (0, 0)),
        ],
        out_specs=pl.BlockSpec((tm, f), lambda i: (i, 0)),
        out_shape=jax.ShapeDtypeStruct((t, f), F32),
        compiler_params=_cparams(("parallel",)),
        name="norm_mod_proj",
    )(x, g.reshape(1, D_MODEL), sc, sh, w)


def _ret_kernel(lg_ref, q_ref, k_ref, v_ref, g_ref, cos_ref, sin_ref, rg_ref, s0_ref, *rest,
                chunk, real):
    o_ref, sout_ref, s_scr = rest[-3:]
    c = pl.program_id(1)

    @pl.when(c == 0)
    def _():
        s_scr[...] = s0_ref[...]

    cos = cos_ref[...]
    sin = sin_ref[...]

    def rope(x):
        return x * cos + pltpu.roll(x, RET_DK // 2, 1) * sin

    ri = lax.broadcasted_iota(I32, (chunk, chunk), 0)
    ci = lax.broadcasted_iota(I32, (chunk, chunk), 1)
    causal = ri >= ci
    rel = jnp.where(causal, ri - ci, 0).astype(F32)
    pos = lax.broadcasted_iota(I32, (chunk, RET_DK), 0).astype(F32)

    for hh in range(RET_HEADS):
        cols = slice(hh * RET_DK, (hh + 1) * RET_DK)
        lg = lg_ref[hh]
        decay = jnp.where(causal, jnp.exp(lg * rel), 0.0)
        in_decay = jnp.exp(lg * (pos + 1.0))
        st_decay = jnp.exp(lg * (real - 1.0 - pos))
        ch_decay = jnp.exp(jnp.full((1, RET_DK), real, F32) * lg)
        q = rope(q_ref[:, cols])
        k = rope(k_ref[:, cols]) * (RET_DK ** -0.5)
        v = v_ref[:, cols]
        s = s_scr[hh]
        att = lax.dot_general(q, k, (((1,), (1,)), ((), ())), preferred_element_type=F32) * decay
        o = (jnp.dot(att, v, preferred_element_type=F32)
             + jnp.dot(q * in_decay, s, preferred_element_type=F32))
        s_scr[hh] = s * ch_decay + lax.dot_general(k * st_decay, v, (((0,), (0,)), ((), ())),
                                                   preferred_element_type=F32)
        y = o * lax.rsqrt(jnp.mean(o * o, axis=-1, keepdims=True) + NORM_EPS) * rg_ref[:, cols]
        o_ref[:, cols] = _silu(g_ref[:, cols]) * y

    @pl.when(c == pl.num_programs(1) - 1)
    def _():
        sout_ref[...] = s_scr[...]


def _state_io(states, layer, prev_out, n_leading_inputs):
    spec = pl.BlockSpec((None, None) + states.shape[2:], lambda b, c, *_: (layer, b, 0, 0, 0))
    extra_specs, extra_args, aliases = [], [], {}
    if prev_out is not None:
        extra_specs = [pl.BlockSpec(memory_space=pl.ANY)]
        extra_args = [prev_out]
        aliases = {n_leading_inputs: 1}
    return spec, extra_specs, extra_args, aliases


def _retention(proj, cos, sin, log_gamma, ret_g, states, layer, prev_out, n, rows, real):
    chunk = min(rows, RET_CHUNK)
    nc = rows // chunk
    h = RET_HEADS

    def col(j):
        return pl.BlockSpec((chunk, RET_WIDTH), lambda b, c, *_: (b * nc + c, j))

    tab = pl.BlockSpec((chunk, RET_DK), lambda b, c, *_: (c, 0))
    st, extra_specs, extra_args, aliases = _state_io(states, layer, prev_out, 9)
    grid_spec = pltpu.PrefetchScalarGridSpec(
        num_scalar_prefetch=1,
        grid=(n, nc),
        in_specs=[col(0), col(1), col(2), col(3), tab, tab,
                  pl.BlockSpec((1, RET_WIDTH), lambda b, c, *_: (0, 0)), st] + extra_specs,
        out_specs=[col(0), st],
        scratch_shapes=[pltpu.VMEM((h, RET_DK, RET_DK), F32)],
    )
    return pl.pallas_call(
        functools.partial(_ret_kernel, chunk=chunk, real=float(real if nc == 1 else chunk)),
        grid_spec=grid_spec,
        out_shape=[jax.ShapeDtypeStruct((n * rows, RET_WIDTH), F32),
                   jax.ShapeDtypeStruct(states.shape, F32)],
        input_output_aliases=aliases,
        compiler_params=_cparams(("parallel", "arbitrary")),
        name="retention",
    )(log_gamma, proj, proj, proj, proj, cos, sin, ret_g.reshape(1, RET_WIDTH), states, *extra_args)


def _conv_kernel(bg_ref, cg_ref, xi_ref, w_ref, zp_ref, o_ref, tail_ref, prev_scr, *, chunk):
    c = pl.program_id(1)

    @pl.when(c == 0)
    def _():
        prev_scr[...] = zp_ref[...]

    z = cg_ref[...] * xi_ref[...]
    p2 = prev_scr[SUBLANES - 2:SUBLANES - 1, :]
    p1 = prev_scr[SUBLANES - 1:SUBLANES, :]
    row = lax.broadcasted_iota(I32, (chunk, CONV_DIM), 0)
    z1 = jnp.where(row == 0, p1, pltpu.roll(z, 1, 0))
    z2 = jnp.where(row == 0, p2, jnp.where(row == 1, p1, pltpu.roll(z, 2, 0)))
    conv = w_ref[0:1, :] * z2 + w_ref[1:2, :] * z1 + w_ref[2:3, :] * z
    o_ref[...] = bg_ref[...] * conv
    tail = z[chunk - SUBLANES:, :]
    prev_scr[...] = tail
    tail_ref[...] = tail


def _conv(proj, conv_w, zprev8, n, rows):
    chunk = min(rows, TOKEN_BLOCK)
    nc = rows // chunk
    base = 4 * RET_WIDTH // CONV_DIM

    def col(j):
        return pl.BlockSpec((chunk, CONV_DIM), lambda b, c: (b * nc + c, base + j))

    return pl.pallas_call(
        functools.partial(_conv_kernel, chunk=chunk),
        grid=(n, nc),
        in_specs=[col(0), col(1), col(2),
                  pl.BlockSpec((3, CONV_DIM), lambda b, c: (0, 0)),
                  pl.BlockSpec((None, SUBLANES, CONV_DIM), lambda b, c: (b, 0, 0))],
        out_specs=[pl.BlockSpec((chunk, CONV_DIM), lambda b, c: (b * nc + c, 0)),
                   pl.BlockSpec((None, SUBLANES, CONV_DIM), lambda b, c: (b, 0, 0))],
        out_shape=[jax.ShapeDtypeStruct((n * rows, CONV_DIM), F32),
                   jax.ShapeDtypeStruct((n, SUBLANES, CONV_DIM), F32)],
        scratch_shapes=[pltpu.VMEM((SUBLANES, CONV_DIM), F32)],
        compiler_params=_cparams(("parallel", "arbitrary")),
        name="short_conv",
    )(proj, proj, proj, conv_w, zprev8)


def _hgrn_kernel(q_ref, f_ref, i_ref, g_ref, lb_ref, ng_ref, s0_ref, *rest, block, sub, real):
    o_ref, sout_ref, st_scr, b_scr, q_scr, k_scr, o_scr = rest[-7:]
    c = pl.program_id(1)

    @pl.when(c == 0)
    def _():
        for hh in range(HG_HEADS):
            st_scr[hh] = s0_ref[hh].T

    lb = lb_ref[...]
    fg = lb + (1.0 - lb) * _sigmoid(f_ref[...])
    k_scr[...] = 1.0 - fg
    q_scr[...] = _silu(q_ref[...])
    b = jnp.log(fg)
    rowm = lax.broadcasted_iota(I32, b.shape, 0) & (sub - 1)
    step = 1
    while step < sub:
        b = b + jnp.where(rowm >= step, pltpu.roll(b, step, 0), 0.0)
        step *= 2
    b_scr[...] = b

    ones = jnp.ones((HG_DK, HG_DK), BF16)
    rowi = lax.broadcasted_iota(I32, (sub, HG_DK), 0)
    nsub = block // sub
    last = (real - 1) if nsub == 1 else (sub - 1)

    def body(sb, carry):
        rows = pl.ds(pl.multiple_of(sb * sub, sub), sub)
        for hh in range(HG_HEADS):
            cols = slice(hh * HG_DK, (hh + 1) * HG_DK)
            b = b_scr[rows, cols]
            q = q_scr[rows, cols]
            kk = k_scr[rows, cols]
            iv = i_ref[rows, cols]
            st = st_scr[hh]
            o = lax.dot_general((q * jnp.exp(b)).astype(BF16), st.astype(BF16),
                                (((1,), (1,)), ((), ())), preferred_element_type=F32)
            parts = []
            for j in range(sub):
                e = jnp.exp(jnp.where(rowi >= j, b - b[j:j + 1, :], -jnp.inf))
                parts.append(q * kk[j:j + 1, :] * e)
            t_all = jnp.concatenate(parts, axis=0).astype(BF16)
            s_all = jnp.dot(t_all, ones, preferred_element_type=F32)
            for j in range(sub):
                o = o + s_all[j * sub:(j + 1) * sub, :] * iv[j:j + 1, :]
            b_end = b[last:last + 1, :]
            kd = (kk * jnp.exp(b_end - b)).astype(BF16)
            st_scr[hh] = st * jnp.exp(b_end) + lax.dot_general(
                iv.astype(BF16), kd, (((0,), (0,)), ((), ())), preferred_element_type=F32)
            o_scr[rows, cols] = o
        return carry

    lax.fori_loop(0, nsub, body, 0)

    for hh in range(HG_HEADS):
        cols = slice(hh * HG_DK, (hh + 1) * HG_DK)
        o = o_scr[:, cols]
        y = o * lax.rsqrt(jnp.mean(o * o, axis=-1, keepdims=True) + NORM_EPS) * ng_ref[:, cols]
        o_ref[:, cols] = y * _silu(g_ref[:, cols])

    @pl.when(c == pl.num_programs(1) - 1)
    def _():
        for hh in range(HG_HEADS):
            sout_ref[hh] = st_scr[hh].T


def _hgrn(proj, lb, norm_g, states, layer, prev_out, n, rows, real):
    block = min(rows, HG_BLOCK)
    nc = rows // block
    sub = min(HG_SUB, block)
    h = HG_HEADS
    width = h * HG_DK

    def col(j):
        return pl.BlockSpec((block, width), lambda b, c: (b * nc + c, j))

    vec = pl.BlockSpec((1, width), lambda b, c: (0, 0))
    st, extra_specs, extra_args, aliases = _state_io(states, layer, prev_out, 7)
    return pl.pallas_call(
        functools.partial(_hgrn_kernel, block=block, sub=sub, real=real),
        grid=(n, nc),
        in_specs=[col(0), col(1), col(2), col(3), vec, vec, st] + extra_specs,
        out_specs=[col(0), st],
        out_shape=[jax.ShapeDtypeStruct((n * rows, width), F32),
                   jax.ShapeDtypeStruct(states.shape, F32)],
        input_output_aliases=aliases,
        scratch_shapes=[pltpu.VMEM((h, HG_DK, HG_DK), F32)] + [pltpu.VMEM((block, width), F32)] * 4,
        compiler_params=_cparams(("parallel", "arbitrary")),
        name="hgrn2",
    )(proj, proj, proj, proj, lb.reshape(1, width), norm_g.reshape(1, width), states, *extra_args)


def _outproj_kernel(*refs, n_in):
    x_ref, gate_ref = refs[0], refs[1]
    a_refs = refs[2:2 + n_in]
    w_refs = refs[2 + n_in:2 + 2 * n_in]
    o_ref = refs[2 + 2 * n_in]
    y = jnp.dot(a_refs[0][...].astype(BF16), w_refs[0][...], preferred_element_type=F32)
    for a_ref, w_ref in zip(a_refs[1:], w_refs[1:]):
        y = y + jnp.dot(a_ref[...].astype(BF16), w_ref[...], preferred_element_type=F32)
    o_ref[...] = x_ref[...] + gate_ref[...] * y


def _outproj(x, gate, acts, ws, per_token, rows_per_seq):
    t = x.shape[0]
    tm = min(TOKEN_BLOCK, t)
    n_in = len(acts)
    in_specs = [pl.BlockSpec((tm, D_MODEL), lambda i: (i, 0)), _mod_spec(per_token, tm, rows_per_seq)]
    in_specs += [pl.BlockSpec((tm, a.shape[1]), lambda i: (i, 0)) for a in acts]
    in_specs += [pl.BlockSpec(w.shape, lambda i: (0, 0)) for w in ws]
    return pl.pallas_call(
        functools.partial(_outproj_kernel, n_in=n_in),
        grid=(t // tm,),
        in_specs=in_specs,
        out_specs=pl.BlockSpec((tm, D_MODEL), lambda i: (i, 0)),
        out_shape=jax.ShapeDtypeStruct((t, D_MODEL), F32),
        compiler_params=_cparams(("parallel",)),
        name="out_proj",
    )(x, gate, *acts, *ws)


def _load(ref):
    return ref[...]


def _topk16(problems):
    def body(k, carry):
        for v_ref, code_fn, val_out, code_out in problems:
            v = v_ref[...]
            code = code_fn()
            m = jnp.max(v, axis=0, keepdims=True)
            cmin = jnp.min(jnp.where(v == m, code, jnp.inf), axis=0, keepdims=True)
            v_ref[...] = jnp.where(code == cmin, -jnp.inf, v)
            val_out[pl.ds(k, 1), :] = m
            code_out[pl.ds(k, 1), :] = cmin
        return carry

    lax.fori_loop(0, PEER_TOPK, body, 0)


def _topk16_paired(problems):
    def body(k, carry):
        for hi_ref, lo_ref, chi_ref, clo_ref, val_out, code_out in problems:
            hi = hi_ref[...]
            chi = chi_ref[...]
            lo = lo_ref[...]
            m = jnp.max(hi, axis=0, keepdims=True)
            cmin = jnp.min(jnp.where(hi == m, chi, jnp.inf), axis=0, keepdims=True)
            took = chi == cmin
            hi_ref[...] = jnp.where(took, lo, hi)
            chi_ref[...] = jnp.where(took, clo_ref[...], chi)
            lo_ref[...] = jnp.where(took, -jnp.inf, lo)
            val_out[pl.ds(k, 1), :] = m
            code_out[pl.ds(k, 1), :] = cmin
        return carry

    lax.fori_loop(0, PEER_TOPK, body, 0)


def _route_kernel(x_ref, g_ref, sc_ref, sh_ref, wq_ref, keys_ref, e1_ref, e2_ref, gt_ref,
                  q_scr, hi_scr, lo_scr, chi_scr, clo_scr, comb_scr, code_scr, v_scr, i_scr,
                  vs_scr, cs_scr, e1t_scr, e2t_scr, gtt_scr, *, tr):
    h = _norm_mod(x_ref[...], g_ref[...], sc_ref[...], sh_ref[...]).astype(BF16)
    q = jnp.dot(h, wq_ref[...], preferred_element_type=F32).astype(BF16)
    for hp in range(2 * PEER_HEADS):
        q_scr[hp] = q[:, hp * N_KEYS:(hp + 1) * N_KEYS]

    kk = PEER_TOPK
    nexp = float(N_KEYS * N_KEYS)
    half = N_KEYS // 2
    key_lo = lax.broadcasted_iota(I32, (half, tr), 0).astype(F32)

    def heads(it, carry):
        first = []
        for u in range(2 * ROUTE_HEADS):
            hp = 2 * ROUTE_HEADS * it + u
            s = lax.dot_general(keys_ref[hp], q_scr[hp], (((1,), (1,)), ((), ())),
                                preferred_element_type=F32)
            a, b = s[:half, :], s[half:, :]
            ge = a >= b
            hi_scr[u] = jnp.where(ge, a, b)
            lo_scr[u] = jnp.where(ge, b, a)
            chi_scr[u] = jnp.where(ge, key_lo, key_lo + half)
            clo_scr[u] = jnp.where(ge, key_lo + half, key_lo)
            first.append((hi_scr.at[u], lo_scr.at[u], chi_scr.at[u], clo_scr.at[u],
                          v_scr.at[u], i_scr.at[u]))
        _topk16_paired(first)
        second = []
        for u in range(ROUTE_HEADS):
            v1_ref, i1_ref = v_scr.at[2 * u], i_scr.at[2 * u]
            v2 = v_scr[2 * u + 1]
            i2 = i_scr[2 * u + 1]
            comb, code = comb_scr.at[u], code_scr.at[u]
            off = 0
            for a in range(kk // 2):
                nb = kk // (a + 1)
                rows = -(-nb // SUBLANES) * SUBLANES
                bpos = lax.broadcasted_iota(I32, (rows, tr), 0)
                val = v1_ref[a:a + 1, :] + v2[0:rows, :]
                if nb < rows:
                    val = jnp.where(bpos < nb, val, -jnp.inf)
                comb[off:off + rows, :] = val
                code[off:off + rows, :] = ((bpos + a * kk).astype(F32) * nexp
                                           + (i1_ref[a:a + 1, :] * N_KEYS + i2[0:rows, :]))
                off += rows
            apos = lax.broadcasted_iota(I32, (kk // 2, tr), 0) + kk // 2
            comb[off:off + kk // 2, :] = v1_ref[kk // 2:kk, :] + v2[0:1, :]
            code[off:off + kk // 2, :] = ((apos * kk).astype(F32) * nexp
                                          + (i1_ref[kk // 2:kk, :] * N_KEYS + i2[0:1, :]))
            second.append((comb, functools.partial(_load, code), vs_scr.at[u], cs_scr.at[u]))
        _topk16(second)
        for u in range(ROUTE_HEADS):
            vals = vs_scr[u]
            ex = jnp.exp(vals - vals[0:1, :])
            gate = ex / jnp.sum(ex, axis=0, keepdims=True)
            eid = cs_scr[u].astype(I32) & (N_KEYS * N_KEYS - 1)
            rows = pl.ds(pl.multiple_of((ROUTE_HEADS * it + u) * kk, kk), kk)
            e1t_scr[rows, :] = (eid >> 7).astype(F32)
            e2t_scr[rows, :] = (eid & (N_KEYS - 1)).astype(F32)
            gtt_scr[rows, :] = gate
        return carry

    lax.fori_loop(0, PEER_HEADS // ROUTE_HEADS, heads, 0)
    e1_ref[...] = e1t_scr[...].T
    e2_ref[...] = e2t_scr[...].T
    gt_ref[...] = gtt_scr[...].T


def _route(x, g, sc, sh, wq, keys, per_token, rows_per_seq):
    t = x.shape[0]
    tr = min(ROUTE_BLOCK, t)
    nslot = PEER_HEADS * PEER_TOPK
    mspec = _mod_spec(per_token, tr, rows_per_seq)
    out = pl.BlockSpec((tr, nslot), lambda i: (i, 0))
    kk = PEER_TOPK
    rh = ROUTE_HEADS
    ncand = sum(-(-(kk // (a + 1)) // SUBLANES) * SUBLANES for a in range(kk // 2)) + kk // 2
    return pl.pallas_call(
        functools.partial(_route_kernel, tr=tr),
        grid=(t // tr,),
        in_specs=[pl.BlockSpec((tr, D_MODEL), lambda i: (i, 0)),
                  pl.BlockSpec((1, D_MODEL), lambda i: (0, 0)),
                  mspec, mspec,
                  pl.BlockSpec(wq.shape, lambda i: (0, 0)),
                  pl.BlockSpec(keys.shape, lambda i: (0, 0, 0))],
        out_specs=[out, out, out],
        out_shape=[jax.ShapeDtypeStruct((t, nslot), F32)] * 3,
        scratch_shapes=[pltpu.VMEM((2 * PEER_HEADS, tr, N_KEYS), BF16)]
                       + [pltpu.VMEM((2 * rh, N_KEYS // 2, tr), F32)] * 4
                       + [pltpu.VMEM((rh, ncand, tr), F32)] * 2
                       + [pltpu.VMEM((2 * rh, kk, tr), F32)] * 2
                       + [pltpu.VMEM((rh, kk, tr), F32)] * 2
                       + [pltpu.VMEM((nslot, tr), F32)] * 3,
        compiler_params=_cparams(("parallel",)),
        name="peer_route",
    )(x, g.reshape(1, D_MODEL), sc, sh, wq, keys)


def _gelu(x):
    return 0.5 * x * (1.0 + lax.erf(x * (2.0 ** -0.5)))


def _experts_kernel(x_ref, g_ref, sc_ref, sh_ref, gg_ref, e1_ref, e2_ref, gt_ref, u_ref, v_ref,
                    o_ref, h_scr, w_scr, acc_scr, p_scr, *, ts):
    c = pl.program_id(1)
    nsl = N_KEYS // 2

    @pl.when(c == 0)
    def _():
        h_scr[...] = _norm_mod(x_ref[...], g_ref[...], sc_ref[...], sh_ref[...]).astype(BF16)
        rho = lax.broadcasted_iota(I32, (N_KEYS, N_KEYS), 0).astype(F32).astype(BF16)
        one = jnp.ones((N_KEYS, N_KEYS), BF16)
        zero = jnp.zeros((N_KEYS, N_KEYS), BF16)

        def pair_words(x):
            b = pltpu.bitcast(x.astype(BF16).astype(F32), U32) >> 16
            return b | (b << 16)

        def body(tb, carry):
            base = pl.multiple_of(tb * SUBLANES, SUBLANES)
            e1w = pair_words(e1_ref[pl.ds(base, SUBLANES), :])
            e2w = pair_words(e2_ref[pl.ds(base, SUBLANES), :])
            gtw = pair_words(gt_ref[pl.ds(base, SUBLANES), :])
            for i in range(SUBLANES):
                def rows(w):
                    return pltpu.bitcast(jnp.broadcast_to(w[i:i + 1, :], (nsl, N_KEYS)), BF16)

                r1 = jnp.where(rho == rows(e1w), one, zero)
                r2 = jnp.where(rho == rows(e2w), rows(gtw), zero)
                wt = lax.dot_general(r1, r2, (((1,), (1,)), ((), ())), preferred_element_type=F32)
                w_scr[pl.ds(base + i, nsl, stride=W_PITCH), :] = pltpu.bitcast(wt.astype(BF16), U32)
            return carry

        lax.fori_loop(0, ts // SUBLANES, body, 0, unroll=4)

    hmat = h_scr[...]
    for s in range(SUBLANES):
        a2 = lax.dot_general(hmat, u_ref[s * 256:(s + 1) * 256, :], (((1,), (1,)), ((), ())),
                             preferred_element_type=F32)
        start = pl.multiple_of((c * SUBLANES + s) * W_PITCH, SUBLANES)
        packed = w_scr[pl.ds(start, ts), :]
        lo = pltpu.bitcast(packed << 16, F32)
        hi = pltpu.bitcast(packed & jnp.uint32(0xFFFF0000), F32)
        w2 = jnp.concatenate([lo, hi], axis=1)
        p_scr[:, s * 256:(s + 1) * 256] = (_gelu(a2) * w2).astype(BF16)
    part = jnp.dot(p_scr[...], v_ref[...], preferred_element_type=F32)

    @pl.when(c == 0)
    def _():
        acc_scr[...] = part

    @pl.when(c > 0)
    def _():
        acc_scr[...] += part

    @pl.when(c == pl.num_programs(1) - 1)
    def _():
        o_ref[...] = x_ref[...] + gg_ref[...] * acc_scr[...]


def _experts(x, g, sc, sh, gg, e1, e2, gt, u, v, per_token, rows_per_seq):
    t = x.shape[0]
    ts = EXPERT_ROWS
    assert t % ts == 0
    nslot = PEER_HEADS * PEER_TOPK
    mspec = _mod_spec(per_token, ts, rows_per_seq)
    slot = pl.BlockSpec((ts, nslot), lambda i, c: (i, 0))
    tab = pl.BlockSpec((EXPERT_CHUNK, D_MODEL), lambda i, c: (c, 0))
    return pl.pallas_call(
        functools.partial(_experts_kernel, ts=ts),
        grid=(t // ts, N_KEYS * N_KEYS // EXPERT_CHUNK),
        in_specs=[pl.BlockSpec((ts, D_MODEL), lambda i, c: (i, 0)),
                  pl.BlockSpec((1, D_MODEL), lambda i, c: (0, 0)),
                  mspec, mspec, mspec, slot, slot, slot, tab, tab],
        out_specs=pl.BlockSpec((ts, D_MODEL), lambda i, c: (i, 0)),
        out_shape=jax.ShapeDtypeStruct((t, D_MODEL), F32),
        scratch_shapes=[pltpu.VMEM((ts, D_MODEL), BF16),
                        pltpu.VMEM((N_KEYS // 2 * W_PITCH, LANES), U32),
                        pltpu.VMEM((ts, D_MODEL), F32),
                        pltpu.VMEM((ts, EXPERT_CHUNK), BF16)],
        compiler_params=_cparams(("parallel", "arbitrary")),
        name="peer_experts",
    )(x, g.reshape(1, D_MODEL), sc, sh, gg, e1, e2, gt, u, v)


def _final_kernel(x_ref, g_ref, o_ref):
    x = x_ref[...]
    o_ref[...] = x * lax.rsqrt(jnp.mean(x * x, axis=-1, keepdims=True) + NORM_EPS) * g_ref[...]


def _final_norm(x, g):
    t = x.shape[0]
    tm = min(TOKEN_BLOCK, t)
    return pl.pallas_call(
        _final_kernel,
        grid=(t // tm,),
        in_specs=[pl.BlockSpec((tm, D_MODEL), lambda i: (i, 0)),
                  pl.BlockSpec((1, D_MODEL), lambda i: (0, 0))],
        out_specs=pl.BlockSpec((tm, D_MODEL), lambda i: (i, 0)),
        out_shape=jax.ShapeDtypeStruct((t, D_MODEL), F32),
        compiler_params=_cparams(("parallel",)),
        name="final_norm",
    )(x, g.reshape(1, D_MODEL))


def _pad_rows(a, n, length, rows):
    if rows == length:
        return a
    a = a.reshape(n, length, a.shape[-1])
    a = jnp.pad(a, ((0, 0), (0, rows - length), (0, 0)))
    return a.reshape(n * rows, a.shape[-1])


def _unpad_rows(a, n, length, rows):
    if rows == length:
        return a
    return a.reshape(n, rows, a.shape[-1])[:, :length].reshape(n * length, a.shape[-1])


def _trunk(x, mod, pos0, ret_s, conv_s, hg_s, w):
    n, length, _ = x.shape
    t = n * length
    per_token = length < TOKEN_BLOCK
    rows = max(length, SUBLANES)
    x = x.reshape(t, D_MODEL)

    pos = pos0 + jnp.arange(rows, dtype=F32)
    half = RET_DK // 2
    inv = 1.0 / (ROPE_BASE ** jnp.linspace(0.0, 1.0, half, dtype=F32))
    ang = pos[:, None] * inv[None, :]
    cos = jnp.concatenate([jnp.cos(ang), jnp.cos(ang)], axis=-1)
    sin = jnp.concatenate([-jnp.sin(ang), jnp.sin(ang)], axis=-1)
    log_gamma = jnp.log(1.0 - 2.0 ** (-5.0 - jnp.arange(RET_HEADS, dtype=F32)))
    sm = jax.nn.softmax(w['hg_lower_bounds'].astype(F32), axis=0)
    lbs = jnp.cumsum(sm, axis=0) - sm[0]

    def mods(l):
        parts = jnp.split(mod[l], 6, axis=-1)
        if per_token:
            return [jnp.repeat(p, length, axis=0) for p in parts]
        return [p.reshape(n, 1, D_MODEL) for p in parts]

    new_ret, new_conv, new_hg = None, [], None
    for l in range(DEPTH):
        sh1, sc1, g1, sh2, sc2, g2 = mods(l)
        j = l // 2
        if l % 2 == 0:
            proj = _norm_mod_matmul(x, w['norm_mix_g'][l], sc1, sh1, w['ab_w_in'][j], per_token, length)
            projp = _pad_rows(proj, n, length, rows)
            ret_out, new_ret = _retention(projp, cos, sin, log_gamma, w['ret_norm_g'][j], ret_s, j,
                                          new_ret, n, rows, length)
            zprev8 = jnp.pad(conv_s[j], ((0, 0), (SUBLANES - 2, 0), (0, 0)))
            conv_out, tail = _conv(projp, w['conv_w'][j], zprev8, n, rows)
            r0 = (length - 2) % SUBLANES if rows == length else length - 2
            new_conv.append(tail[:, r0:r0 + 2, :])
            acts = [_unpad_rows(ret_out, n, length, rows), _unpad_rows(conv_out, n, length, rows)]
            wo = w['ab_w_out'][j]
            x = _outproj(x, g1, acts, [wo[:RET_WIDTH], wo[RET_WIDTH:]], per_token, length)
        else:
            proj = _norm_mod_matmul(x, w['norm_mix_g'][l], sc1, sh1, w['hg_w_in'][j], per_token, length)
            projp = _pad_rows(proj, n, length, rows)
            o, new_hg = _hgrn(projp, lbs[l], w['hg_norm_g'][j], hg_s, j, new_hg, n, rows, length)
            x = _outproj(x, g1, [_unpad_rows(o, n, length, rows)], [w['hg_w_out'][j]], per_token, length)
        e1, e2, gt = _route(x, w['norm_ffn_g'][l], sc2, sh2, w['peer_wq'][l], w['peer_keys'][l],
                            per_token, length)
        x = _experts(x, w['norm_ffn_g'][l], sc2, sh2, g2, e1, e2, gt, w['peer_u'][l], w['peer_v'][l],
                     per_token, length)
    y = _final_norm(x, w['final_norm_g']).reshape(n, length, D_MODEL)
    return y, new_ret, jnp.stack(new_conv), new_hg


def kernel(x_prompt, x_sample, state_ret, state_conv, state_hgrn, c_prompt, c_sample, ada_w, ada_b, norm_mix_g, norm_ffn_g, ab_w_in, ret_norm_g, conv_w, ab_w_out, hg_w_in, hg_lower_bounds, hg_norm_g, hg_w_out, peer_wq, peer_keys, peer_u, peer_v, final_norm_g):
    nb = x_prompt.shape[0]
    n_even, n_odd = state_ret.shape[0], state_hgrn.shape[0]
    w = dict(
        norm_mix_g=norm_mix_g, norm_ffn_g=norm_ffn_g, ret_norm_g=ret_norm_g, conv_w=conv_w,
        hg_lower_bounds=hg_lower_bounds, hg_norm_g=hg_norm_g, final_norm_g=final_norm_g,
        ab_w_in=ab_w_in.astype(BF16), ab_w_out=ab_w_out.astype(BF16),
        hg_w_in=hg_w_in.astype(BF16), hg_w_out=hg_w_out.astype(BF16),
        peer_wq=peer_wq.astype(BF16),
        peer_keys=peer_keys.astype(BF16).reshape(DEPTH, 2 * PEER_HEADS, N_KEYS, N_KEYS),
        peer_u=peer_u.astype(BF16), peer_v=peer_v.astype(BF16),
    )
    mod = _ada(jnp.concatenate([c_prompt, c_sample], axis=0), ada_w, ada_b)
    mod_p, mod_s = mod[:, :nb], mod[:, nb:]
    dt = x_prompt.dtype
    ret0 = jnp.zeros((n_even, nb) + state_ret.shape[2:], dt)
    conv0 = jnp.zeros((n_even, nb) + state_conv.shape[2:], dt)
    hg0 = jnp.zeros((n_odd, nb) + state_hgrn.shape[2:], dt)
    y_p, ret_p, conv_p, hg_p = _trunk(x_prompt, mod_p, 0.0, ret0, conv0, hg0, w)
    y_s, ret_s, conv_s, hg_s = _trunk(x_sample, mod_s, float(PAST_LEN), state_ret, state_conv, state_hgrn, w)
    return (y_p, y_s, ret_p, conv_p, hg_p, ret_s, conv_s, hg_s)
```

```python
import functools
import math

import jax
import jax.numpy as jnp
from jax import lax
from jax.experimental import pallas as pl
from jax.experimental.pallas import tpu as pltpu

F32 = jnp.float32
BF16 = jnp.bfloat16
I32 = jnp.int32
U32 = jnp.uint32

D_MODEL = 1024
DEPTH = 4
PAST_LEN = 16384
RET_HEADS = 4
RET_DK = 128
RET_WIDTH = RET_HEADS * RET_DK
RET_CHUNK = 128
ROPE_BASE = 10000.0
CONV_DIM = D_MODEL // 2
HG_HEADS = 8
HG_DK = 128
HG_SUB = 16
HG_BLOCK = 256
PEER_HEADS = 8
N_KEYS = 128
PEER_TOPK = 16
NORM_EPS = 1e-6

LANES = 128
SUBLANES = 8
VMEM_LIMIT = 56 * 1024 * 1024

TOKEN_BLOCK = 512
ROUTE_BLOCK = 256
ROUTE_HEADS = 4
EXPERT_ROWS = 512
EXPERT_CHUNK = 2048
W_PITCH = EXPERT_ROWS + SUBLANES


def _cparams(sem):
    return pltpu.CompilerParams(dimension_semantics=sem, vmem_limit_bytes=VMEM_LIMIT)


def _mod_spec(per_token, tm, rows_per_seq, block_of=lambda i: i):
    if per_token:
        return pl.BlockSpec((tm, D_MODEL), lambda i, *_: (block_of(i), 0))
    return pl.BlockSpec((None, 1, D_MODEL), lambda i, *_: (block_of(i) * tm // rows_per_seq, 0, 0))


def _norm_mod(x, g, sc, sh):
    y = x * lax.rsqrt(jnp.mean(x * x, axis=-1, keepdims=True) + NORM_EPS)
    return (y * g) * (1.0 + sc) + sh


def _silu(x):
    return x * (1.0 / (1.0 + jnp.exp(-x)))


def _sigmoid(x):
    return 1.0 / (1.0 + jnp.exp(-x))


def _ada_kernel(c_ref, w_ref, b_ref, o_ref):
    a = _silu(c_ref[...]).astype(BF16)
    o_ref[...] = jnp.dot(a, w_ref[...].astype(BF16), preferred_element_type=F32) + b_ref[...]


def _ada(c, ada_w, ada_b):
    n = c.shape[0]
    tn = 1536
    nj = 6 * D_MODEL // tn
    return pl.pallas_call(
        _ada_kernel,
        grid=(DEPTH, nj),
        in_specs=[
            pl.BlockSpec((n, D_MODEL), lambda l, j: (0, 0)),
            pl.BlockSpec((None, D_MODEL, tn), lambda l, j: (l, 0, j)),
            pl.BlockSpec((None, 1, tn), lambda l, j: (l, 0, j)),
        ],
        out_specs=pl.BlockSpec((None, n, tn), lambda l, j: (l, 0, j)),
        out_shape=jax.ShapeDtypeStruct((DEPTH, n, 6 * D_MODEL), F32),
        compiler_params=_cparams(("parallel", "parallel")),
        name="ada_mod",
    )(c, ada_w, ada_b.reshape(DEPTH, 1, 6 * D_MODEL))


def _nmm_kernel(x_ref, g_ref, sc_ref, sh_ref, w_ref, o_ref, *, tn):
    h = _norm_mod(x_ref[...], g_ref[...], sc_ref[...], sh_ref[...]).astype(BF16)
    for j in range(w_ref.shape[1] // tn):
        cols = slice(j * tn, (j + 1) * tn)
        o_ref[:, cols] = jnp.dot(h, w_ref[:, cols], preferred_element_type=F32)


def _norm_mod_matmul(x, g, sc, sh, w, per_token, rows_per_seq):
    t = x.shape[0]
    f = w.shape[1]
    tm = min(TOKEN_BLOCK, t)
    mspec = _mod_spec(per_token, tm, rows_per_seq)
    return pl.pallas_call(
        functools.partial(_nmm_kernel, tn=512),
        grid=(t // tm,),
        in_specs=[
            pl.BlockSpec((tm, D_MODEL), lambda i: (i, 0)),
            pl.BlockSpec((1, D_MODEL), lambda i: (0, 0)),
            mspec, mspec,
            pl.BlockSpec((D_MODEL, f), lambda i: (0, 0)),
        ],
        out_specs=pl.BlockSpec((tm, f), lambda i: (i, 0)),
        out_shape=jax.ShapeDtypeStruct((t, f), F32),
        compiler_params=_cparams(("parallel",)),
        name="norm_mod_proj",
    )(x, g.reshape(1, D_MODEL), sc, sh, w)


def _ret_kernel(lg_ref, q_ref, k_ref, v_ref, g_ref, cos_ref, sin_ref, rg_ref, s0_ref, *rest,
                chunk, real):
    o_ref, sout_ref, s_scr = rest[-3:]
    c = pl.program_id(1)

    @pl.when(c == 0)
    def _():
        s_scr[...] = s0_ref[...]

    cos = cos_ref[...]
    sin = sin_ref[...]

    def rope(x):
        return x * cos + pltpu.roll(x, RET_DK // 2, 1) * sin

    ri = lax.broadcasted_iota(I32, (chunk, chunk), 0)
    ci = lax.broadcasted_iota(I32, (chunk, chunk), 1)
    causal = ri >= ci
    rel = jnp.where(causal, ri - ci, 0).astype(F32)
    pos = lax.broadcasted_iota(I32, (chunk, RET_DK), 0).astype(F32)

    for hh in range(RET_HEADS):
        cols = slice(hh * RET_DK, (hh + 1) * RET_DK)
        lg = lg_ref[hh]
        decay = jnp.where(causal, jnp.exp(lg * rel), 0.0)
        in_decay = jnp.exp(lg * (pos + 1.0))
        st_decay = jnp.exp(lg * (real - 1.0 - pos))
        ch_decay = jnp.exp(jnp.full((1, RET_DK), real, F32) * lg)
        q = rope(q_ref[:, cols])
        k = rope(k_ref[:, cols]) * (RET_DK ** -0.5)
        v = v_ref[:, cols]
        s = s_scr[hh]
        att = lax.dot_general(q, k, (((1,), (1,)), ((), ())), preferred_element_type=F32) * decay
        o = (jnp.dot(att, v, preferred_element_type=F32)
             + jnp.dot(q * in_decay, s, preferred_element_type=F32))
        s_scr[hh] = s * ch_decay + lax.dot_general(k * st_decay, v, (((0,), (0,)), ((), ())),
                                                   preferred_element_type=F32)
        y = o * lax.rsqrt(jnp.mean(o * o, axis=-1, keepdims=True) + NORM_EPS) * rg_ref[:, cols]
        o_ref[:, cols] = _silu(g_ref[:, cols]) * y

    @pl.when(c == pl.num_programs(1) - 1)
    def _():
        sout_ref[...] = s_scr[...]


def _state_io(states, layer, prev_out, n_leading_inputs):
    spec = pl.BlockSpec((None, None) + states.shape[2:], lambda b, c, *_: (layer, b, 0, 0, 0))
    extra_specs, extra_args, aliases = [], [], {}
    if prev_out is not None:
        extra_specs = [pl.BlockSpec(memory_space=pl.ANY)]
        extra_args = [prev_out]
        aliases = {n_leading_inputs: 1}
    return spec, extra_specs, extra_args, aliases


def _retention(proj, cos, sin, log_gamma, ret_g, states, layer, prev_out, n, rows, real):
    chunk = min(rows, RET_CHUNK)
    nc = rows // chunk
    h = RET_HEADS

    def col(j):
        return pl.BlockSpec((chunk, RET_WIDTH), lambda b, c, *_: (b * nc + c, j))

    tab = pl.BlockSpec((chunk, RET_DK), lambda b, c, *_: (c, 0))
    st, extra_specs, extra_args, aliases = _state_io(states, layer, prev_out, 9)
    grid_spec = pltpu.PrefetchScalarGridSpec(
        num_scalar_prefetch=1,
        grid=(n, nc),
        in_specs=[col(0), col(1), col(2), col(3), tab, tab,
                  pl.BlockSpec((1, RET_WIDTH), lambda b, c, *_: (0, 0)), st] + extra_specs,
        out_specs=[col(0), st],
        scratch_shapes=[pltpu.VMEM((h, RET_DK, RET_DK), F32)],
    )
    return pl.pallas_call(
        functools.partial(_ret_kernel, chunk=chunk, real=float(real if nc == 1 else chunk)),
        grid_spec=grid_spec,
        out_shape=[jax.ShapeDtypeStruct((n * rows, RET_WIDTH), F32),
                   jax.ShapeDtypeStruct(states.shape, F32)],
        input_output_aliases=aliases,
        compiler_params=_cparams(("parallel", "arbitrary")),
        name="retention",
    )(log_gamma, proj, proj, proj, proj, cos, sin, ret_g.reshape(1, RET_WIDTH), states, *extra_args)


def _conv_kernel(bg_ref, cg_ref, xi_ref, w_ref, zp_ref, o_ref, tail_ref, prev_scr, *, chunk):
    c = pl.program_id(1)

    @pl.when(c == 0)
    def _():
        prev_scr[...] = zp_ref[...]

    z = cg_ref[...] * xi_ref[...]
    p2 = prev_scr[SUBLANES - 2:SUBLANES - 1, :]
    p1 = prev_scr[SUBLANES - 1:SUBLANES, :]
    row = lax.broadcasted_iota(I32, (chunk, CONV_DIM), 0)
    z1 = jnp.where(row == 0, p1, pltpu.roll(z, 1, 0))
    z2 = jnp.where(row == 0, p2, jnp.where(row == 1, p1, pltpu.roll(z, 2, 0)))
    conv = w_ref[0:1, :] * z2 + w_ref[1:2, :] * z1 + w_ref[2:3, :] * z
    o_ref[...] = bg_ref[...] * conv
    tail = z[chunk - SUBLANES:, :]
    prev_scr[...] = tail
    tail_ref[...] = tail


def _conv(proj, conv_w, zprev8, n, rows):
    chunk = min(rows, TOKEN_BLOCK)
    nc = rows // chunk
    base = 4 * RET_WIDTH // CONV_DIM

    def col(j):
        return pl.BlockSpec((chunk, CONV_DIM), lambda b, c: (b * nc + c, base + j))

    return pl.pallas_call(
        functools.partial(_conv_kernel, chunk=chunk),
        grid=(n, nc),
        in_specs=[col(0), col(1), col(2),
                  pl.BlockSpec((3, CONV_DIM), lambda b, c: (0, 0)),
                  pl.BlockSpec((None, SUBLANES, CONV_DIM), lambda b, c: (b, 0, 0))],
        out_specs=[pl.BlockSpec((chunk, CONV_DIM), lambda b, c: (b * nc + c, 0)),
                   pl.BlockSpec((None, SUBLANES, CONV_DIM), lambda b, c: (b, 0, 0))],
        out_shape=[jax.ShapeDtypeStruct((n * rows, CONV_DIM), F32),
                   jax.ShapeDtypeStruct((n, SUBLANES, CONV_DIM), F32)],
        scratch_shapes=[pltpu.VMEM((SUBLANES, CONV_DIM), F32)],
        compiler_params=_cparams(("parallel", "arbitrary")),
        name="short_conv",
    )(proj, proj, proj, conv_w, zprev8)


def _hgrn_kernel(q_ref, f_ref, i_ref, g_ref, lb_ref, ng_ref, s0_ref, *rest, block, sub, real):
    o_ref, sout_ref, st_scr, b_scr, q_scr, k_scr, o_scr = rest[-7:]
    c = pl.program_id(1)

    @pl.when(c == 0)
    def _():
        for hh in range(HG_HEADS):
            st_scr[hh] = s0_ref[hh].T

    lb = lb_ref[...]
    fg = lb + (1.0 - lb) * _sigmoid(f_ref[...])
    k_scr[...] = 1.0 - fg
    q_scr[...] = _silu(q_ref[...])
    b = jnp.log(fg)
    rowm = lax.broadcasted_iota(I32, b.shape, 0) & (sub - 1)
    step = 1
    while step < sub:
        b = b + jnp.where(rowm >= step, pltpu.roll(b, step, 0), 0.0)
        step *= 2
    b_scr[...] = b

    ones = jnp.ones((HG_DK, HG_DK), BF16)
    rowi = lax.broadcasted_iota(I32, (SUBLANES, HG_DK), 0)
    nsub = block // sub
    last = (real - 1) if nsub == 1 else (sub - 1)

    def body(sb, carry):
        rows = pl.ds(pl.multiple_of(sb * sub, sub), sub)
        for hh in range(HG_HEADS):
            cols = slice(hh * HG_DK, (hh + 1) * HG_DK)
            b = b_scr[rows, cols]
            q = q_scr[rows, cols]
            kk = k_scr[rows, cols]
            iv = i_ref[rows, cols]
            st = st_scr[hh]
            o = lax.dot_general((q * jnp.exp(b)).astype(BF16), st.astype(BF16),
                                (((1,), (1,)), ((), ())), preferred_element_type=F32)
            parts = []
            for j in range(sub):
                for grp in range(j // SUBLANES, sub // SUBLANES):
                    r = slice(grp * SUBLANES, (grp + 1) * SUBLANES)
                    d = b[r, :] - b[j:j + 1, :]
                    if grp == j // SUBLANES:
                        d = jnp.where(rowi >= j - grp * SUBLANES, d, -jnp.inf)
                    parts.append(q[r, :] * kk[j:j + 1, :] * jnp.exp(d))
            t_all = jnp.concatenate(parts, axis=0).astype(BF16)
            s_all = jnp.dot(t_all, ones, preferred_element_type=F32)
            groups = [o[g * SUBLANES:(g + 1) * SUBLANES, :] for g in range(sub // SUBLANES)]
            n = 0
            for j in range(sub):
                for grp in range(j // SUBLANES, sub // SUBLANES):
                    groups[grp] = groups[grp] + s_all[n * SUBLANES:(n + 1) * SUBLANES, :] * iv[j:j + 1, :]
                    n += 1
            o = jnp.concatenate(groups, axis=0)
            b_end = b[last:last + 1, :]
            kd = (kk * jnp.exp(b_end - b)).astype(BF16)
            st_scr[hh] = st * jnp.exp(b_end) + lax.dot_general(
                iv.astype(BF16), kd, (((0,), (0,)), ((), ())), preferred_element_type=F32)
            o_scr[rows, cols] = o
        return carry

    lax.fori_loop(0, nsub, body, 0)

    for hh in range(HG_HEADS):
        cols = slice(hh * HG_DK, (hh + 1) * HG_DK)
        o = o_scr[:, cols]
        y = o * lax.rsqrt(jnp.mean(o * o, axis=-1, keepdims=True) + NORM_EPS) * ng_ref[:, cols]
        o_ref[:, cols] = y * _silu(g_ref[:, cols])

    @pl.when(c == pl.num_programs(1) - 1)
    def _():
        for hh in range(HG_HEADS):
            sout_ref[hh] = st_scr[hh].T


def _hgrn(proj, lb, norm_g, states, layer, prev_out, n, rows, real):
    block = min(rows, HG_BLOCK)
    nc = rows // block
    sub = min(HG_SUB, block)
    h = HG_HEADS
    width = h * HG_DK

    def col(j):
        return pl.BlockSpec((block, width), lambda b, c: (b * nc + c, j))

    vec = pl.BlockSpec((1, width), lambda b, c: (0, 0))
    st, extra_specs, extra_args, aliases = _state_io(states, layer, prev_out, 7)
    return pl.pallas_call(
        functools.partial(_hgrn_kernel, block=block, sub=sub, real=real),
        grid=(n, nc),
        in_specs=[col(0), col(1), col(2), col(3), vec, vec, st] + extra_specs,
        out_specs=[col(0), st],
        out_shape=[jax.ShapeDtypeStruct((n * rows, width), F32),
                   jax.ShapeDtypeStruct(states.shape, F32)],
        input_output_aliases=aliases,
        scratch_shapes=[pltpu.VMEM((h, HG_DK, HG_DK), F32)] + [pltpu.VMEM((block, width), F32)] * 4,
        compiler_params=_cparams(("parallel", "arbitrary")),
        name="hgrn2",
    )(proj, proj, proj, proj, lb.reshape(1, width), norm_g.reshape(1, width), states, *extra_args)


def _outproj_kernel(*refs, n_in):
    x_ref, gate_ref = refs[0], refs[1]
    a_refs = refs[2:2 + n_in]
    w_refs = refs[2 + n_in:2 + 2 * n_in]
    o_ref = refs[2 + 2 * n_in]
    y = jnp.dot(a_refs[0][...].astype(BF16), w_refs[0][...], preferred_element_type=F32)
    for a_ref, w_ref in zip(a_refs[1:], w_refs[1:]):
        y = y + jnp.dot(a_ref[...].astype(BF16), w_ref[...], preferred_element_type=F32)
    o_ref[...] = x_ref[...] + gate_ref[...] * y


def _outproj(x, gate, acts, ws, per_token, rows_per_seq):
    t = x.shape[0]
    tm = min(TOKEN_BLOCK, t)
    n_in = len(acts)
    in_specs = [pl.BlockSpec((tm, D_MODEL), lambda i: (i, 0)), _mod_spec(per_token, tm, rows_per_seq)]
    in_specs += [pl.BlockSpec((tm, a.shape[1]), lambda i: (i, 0)) for a in acts]
    in_specs += [pl.BlockSpec(w.shape, lambda i: (0, 0)) for w in ws]
    return pl.pallas_call(
        functools.partial(_outproj_kernel, n_in=n_in),
        grid=(t // tm,),
        in_specs=in_specs,
        out_specs=pl.BlockSpec((tm, D_MODEL), lambda i: (i, 0)),
        out_shape=jax.ShapeDtypeStruct((t, D_MODEL), F32),
        compiler_params=_cparams(("parallel",)),
        name="out_proj",
    )(x, gate, *acts, *ws)


def _load(ref):
    return ref[...]


def _topk16(problems):
    def body(k, carry):
        for v_ref, code_fn, val_out, code_out in problems:
            v = v_ref[...]
            code = code_fn()
            m = jnp.max(v, axis=0, keepdims=True)
            cmin = jnp.min(jnp.where(v == m, code, jnp.inf), axis=0, keepdims=True)
            v_ref[...] = jnp.where(code == cmin, -jnp.inf, v)
            val_out[pl.ds(k, 1), :] = m
            code_out[pl.ds(k, 1), :] = cmin
        return carry

    lax.fori_loop(0, PEER_TOPK, body, 0)


def _topk16_paired(problems):
    def body(k, carry):
        for hi_ref, lo_ref, chi_ref, clo_ref, val_out, code_out in problems:
            hi = hi_ref[...]
            chi = chi_ref[...]
            lo = lo_ref[...]
            m = jnp.max(hi, axis=0, keepdims=True)
            cmin = jnp.min(jnp.where(hi == m, chi, jnp.inf), axis=0, keepdims=True)
            took = chi == cmin
            hi_ref[...] = jnp.where(took, lo, hi)
            chi_ref[...] = jnp.where(took, clo_ref[...], chi)
            lo_ref[...] = jnp.where(took, -jnp.inf, lo)
            val_out[pl.ds(k, 1), :] = m
            code_out[pl.ds(k, 1), :] = cmin
        return carry

    lax.fori_loop(0, PEER_TOPK, body, 0)


def _route_kernel(x_ref, g_ref, sc_ref, sh_ref, wq_ref, keys_ref, e1_ref, e2_ref, gt_ref,
                  q_scr, hi_scr, lo_scr, chi_scr, clo_scr, comb_scr, code_scr, v_scr, i_scr,
                  vs_scr, cs_scr, e1t_scr, e2t_scr, gtt_scr, *, tr):
    h = _norm_mod(x_ref[...], g_ref[...], sc_ref[...], sh_ref[...]).astype(BF16)
    q = jnp.dot(h, wq_ref[...], preferred_element_type=F32).astype(BF16)
    for hp in range(2 * PEER_HEADS):
        q_scr[hp] = q[:, hp * N_KEYS:(hp + 1) * N_KEYS]

    kk = PEER_TOPK
    nexp = float(N_KEYS * N_KEYS)
    half = N_KEYS // 2
    key_lo = lax.broadcasted_iota(I32, (half, tr), 0).astype(F32)

    def heads(it, carry):
        first = []
        for u in range(2 * ROUTE_HEADS):
            hp = 2 * ROUTE_HEADS * it + u
            s = lax.dot_general(keys_ref[hp], q_scr[hp], (((1,), (1,)), ((), ())),
                                preferred_element_type=F32)
            a, b = s[:half, :], s[half:, :]
            ge = a >= b
            hi_scr[u] = jnp.where(ge, a, b)
            lo_scr[u] = jnp.where(ge, b, a)
            chi_scr[u] = jnp.where(ge, key_lo, key_lo + half)
            clo_scr[u] = jnp.where(ge, key_lo + half, key_lo)
            first.append((hi_scr.at[u], lo_scr.at[u], chi_scr.at[u], clo_scr.at[u],
                          v_scr.at[u], i_scr.at[u]))
        _topk16_paired(first)
        second = []
        for u in range(ROUTE_HEADS):
            v1_ref, i1_ref = v_scr.at[2 * u], i_scr.at[2 * u]
            v2 = v_scr[2 * u + 1]
            i2 = i_scr[2 * u + 1]
            comb, code = comb_scr.at[u], code_scr.at[u]
            off = 0
            for a in range(kk // 2):
                nb = kk // (a + 1)
                rows = -(-nb // SUBLANES) * SUBLANES
                bpos = lax.broadcasted_iota(I32, (rows, tr), 0)
                val = v1_ref[a:a + 1, :] + v2[0:rows, :]
                if nb < rows:
                    val = jnp.where(bpos < nb, val, -jnp.inf)
                comb[off:off + rows, :] = val
                code[off:off + rows, :] = ((bpos + a * kk).astype(F32) * nexp
                                           + (i1_ref[a:a + 1, :] * N_KEYS + i2[0:rows, :]))
                off += rows
            apos = lax.broadcasted_iota(I32, (kk // 2, tr), 0) + kk // 2
            comb[off:off + kk // 2, :] = v1_ref[kk // 2:kk, :] + v2[0:1, :]
            code[off:off + kk // 2, :] = ((apos * kk).astype(F32) * nexp
                                          + (i1_ref[kk // 2:kk, :] * N_KEYS + i2[0:1, :]))
            second.append((comb, functools.partial(_load, code), vs_scr.at[u], cs_scr.at[u]))
        _topk16(second)
        for u in range(ROUTE_HEADS):
            vals = vs_scr[u]
            ex = jnp.exp(vals - vals[0:1, :])
            gate = ex / jnp.sum(ex, axis=0, keepdims=True)
            eid = cs_scr[u].astype(I32) & (N_KEYS * N_KEYS - 1)
            rows = pl.ds(pl.multiple_of((ROUTE_HEADS * it + u) * kk, kk), kk)
            e1t_scr[rows, :] = (eid >> 7).astype(F32)
            e2t_scr[rows, :] = (eid & (N_KEYS - 1)).astype(F32)
            gtt_scr[rows, :] = gate
        return carry

    lax.fori_loop(0, PEER_HEADS // ROUTE_HEADS, heads, 0)
    e1_ref[...] = e1t_scr[...].T
    e2_ref[...] = e2t_scr[...].T
    gt_ref[...] = gtt_scr[...].T


def _route(x, g, sc, sh, wq, keys, per_token, rows_per_seq):
    t = x.shape[0]
    tr = min(ROUTE_BLOCK, t)
    nslot = PEER_HEADS * PEER_TOPK
    mspec = _mod_spec(per_token, tr, rows_per_seq)
    out = pl.BlockSpec((tr, nslot), lambda i: (i, 0))
    kk = PEER_TOPK
    rh = ROUTE_HEADS
    ncand = sum(-(-(kk // (a + 1)) // SUBLANES) * SUBLANES for a in range(kk // 2)) + kk // 2
    return pl.pallas_call(
        functools.partial(_route_kernel, tr=tr),
        grid=(t // tr,),
        in_specs=[pl.BlockSpec((tr, D_MODEL), lambda i: (i, 0)),
                  pl.BlockSpec((1, D_MODEL), lambda i: (0, 0)),
                  mspec, mspec,
                  pl.BlockSpec(wq.shape, lambda i: (0, 0)),
                  pl.BlockSpec(keys.shape, lambda i: (0, 0, 0))],
        out_specs=[out, out, out],
        out_shape=[jax.ShapeDtypeStruct((t, nslot), F32)] * 3,
        scratch_shapes=[pltpu.VMEM((2 * PEER_HEADS, tr, N_KEYS), BF16)]
                       + [pltpu.VMEM((2 * rh, N_KEYS // 2, tr), F32)] * 4
                       + [pltpu.VMEM((rh, ncand, tr), F32)] * 2
                       + [pltpu.VMEM((2 * rh, kk, tr), F32)] * 2
                       + [pltpu.VMEM((rh, kk, tr), F32)] * 2
                       + [pltpu.VMEM((nslot, tr), F32)] * 3,
        compiler_params=_cparams(("parallel",)),
        name="peer_route",
    )(x, g.reshape(1, D_MODEL), sc, sh, wq, keys)


def _gelu(x):
    return 0.5 * x * (1.0 + lax.erf(x * (2.0 ** -0.5)))


def _pair_words(x):
    b = pltpu.bitcast(x.astype(BF16).astype(F32), U32) >> 16
    return b | (b << 16)


def _experts_kernel(x_ref, g_ref, sc_ref, sh_ref, gg_ref, e1_ref, e2_ref, gt_ref, u_ref, v_ref,
                    o_ref, h_scr, w_scr, acc_scr, p_scr, *, ts):
    c = pl.program_id(1)
    nsl = N_KEYS // 2

    @pl.when(c == 0)
    def _():
        h_scr[...] = _norm_mod(x_ref[...], g_ref[...], sc_ref[...], sh_ref[...]).astype(BF16)
        acc_scr[...] = jnp.zeros(acc_scr.shape, F32)
        rho = lax.broadcasted_iota(I32, (N_KEYS, N_KEYS), 0).astype(F32).astype(BF16)
        one = jnp.ones((N_KEYS, N_KEYS), BF16)
        zero = jnp.zeros((N_KEYS, N_KEYS), BF16)

        def body(tb, carry):
            base = pl.multiple_of(tb * SUBLANES, SUBLANES)
            e1w = _pair_words(e1_ref[pl.ds(base, SUBLANES), :])
            e2w = _pair_words(e2_ref[pl.ds(base, SUBLANES), :])
            gtw = _pair_words(gt_ref[pl.ds(base, SUBLANES), :])
            for i in range(SUBLANES):
                def rows(w):
                    return pltpu.bitcast(jnp.broadcast_to(w[i:i + 1, :], (nsl, N_KEYS)), BF16)

                r1 = jnp.where(rho == rows(e1w), one, zero)
                r2 = jnp.where(rho == rows(e2w), rows(gtw), zero)
                wt = lax.dot_general(r1, r2, (((1,), (1,)), ((), ())), preferred_element_type=F32)
                w_scr[pl.ds(base + i, nsl, stride=W_PITCH), :] = pltpu.bitcast(wt.astype(BF16), U32)
            return carry

        lax.fori_loop(0, ts // SUBLANES, body, 0, unroll=8)

    hmat = h_scr[...]
    for s in range(SUBLANES):
        a2 = lax.dot_general(hmat, u_ref[s * 256:(s + 1) * 256, :], (((1,), (1,)), ((), ())),
                             preferred_element_type=F32)
        start = pl.multiple_of((c * SUBLANES + s) * W_PITCH, SUBLANES)
        packed = w_scr[pl.ds(start, ts), :]
        lo = pltpu.bitcast(packed << 16, F32)
        hi = pltpu.bitcast(packed & jnp.uint32(0xFFFF0000), F32)
        w2 = jnp.concatenate([lo, hi], axis=1)
        p_scr[:, s * 256:(s + 1) * 256] = (_gelu(a2) * w2).astype(BF16)
    acc_scr[...] += jnp.dot(p_scr[...], v_ref[...], preferred_element_type=F32)

    @pl.when(c == pl.num_programs(1) - 1)
    def _():
        o_ref[...] = x_ref[...] + gg_ref[...] * acc_scr[...]


def _experts(x, g, sc, sh, gg, e1, e2, gt, u, v, layer, per_token, rows_per_seq):
    t = x.shape[0]
    ts = EXPERT_ROWS
    assert t % ts == 0 and EXPERT_CHUNK == 2 * SUBLANES * N_KEYS
    nslot = PEER_HEADS * PEER_TOPK
    mspec = _mod_spec(per_token, ts, rows_per_seq)
    slot = pl.BlockSpec((ts, nslot), lambda i, c: (i, 0))
    tab = pl.BlockSpec((None, EXPERT_CHUNK, D_MODEL), lambda i, c: (layer, c, 0))
    return pl.pallas_call(
        functools.partial(_experts_kernel, ts=ts),
        grid=(t // ts, N_KEYS * N_KEYS // EXPERT_CHUNK),
        in_specs=[pl.BlockSpec((ts, D_MODEL), lambda i, c: (i, 0)),
                  pl.BlockSpec((1, D_MODEL), lambda i, c: (0, 0)),
                  mspec, mspec, mspec, slot, slot, slot, tab, tab],
        out_specs=pl.BlockSpec((ts, D_MODEL), lambda i, c: (i, 0)),
        out_shape=jax.ShapeDtypeStruct((t, D_MODEL), F32),
        scratch_shapes=[pltpu.VMEM((ts, D_MODEL), BF16),
                        pltpu.VMEM((N_KEYS // 2 * W_PITCH, LANES), U32),
                        pltpu.VMEM((ts, D_MODEL), F32),
                        pltpu.VMEM((ts, EXPERT_CHUNK), BF16)],
        compiler_params=_cparams(("parallel", "arbitrary")),
        name="peer_experts",
    )(x, g.reshape(1, D_MODEL), sc, sh, gg, e1, e2, gt, u, v)


def _final_kernel(x_ref, g_ref, o_ref):
    x = x_ref[...]
    o_ref[...] = x * lax.rsqrt(jnp.mean(x * x, axis=-1, keepdims=True) + NORM_EPS) * g_ref[...]


def _final_norm(x, g):
    t = x.shape[0]
    tm = min(TOKEN_BLOCK, t)
    return pl.pallas_call(
        _final_kernel,
        grid=(t // tm,),
        in_specs=[pl.BlockSpec((tm, D_MODEL), lambda i: (i, 0)),
                  pl.BlockSpec((1, D_MODEL), lambda i: (0, 0))],
        out_specs=pl.BlockSpec((tm, D_MODEL), lambda i: (i, 0)),
        out_shape=jax.ShapeDtypeStruct((t, D_MODEL), F32),
        compiler_params=_cparams(("parallel",)),
        name="final_norm",
    )(x, g.reshape(1, D_MODEL))


def _pad_rows(a, n, length, rows):
    if rows == length:
        return a
    a = a.reshape(n, length, a.shape[-1])
    a = jnp.pad(a, ((0, 0), (0, rows - length), (0, 0)))
    return a.reshape(n * rows, a.shape[-1])


def _unpad_rows(a, n, length, rows):
    if rows == length:
        return a
    return a.reshape(n, rows, a.shape[-1])[:, :length].reshape(n * length, a.shape[-1])


def _trunk(x, mod, pos0, ret_s, conv_s, hg_s, w):
    n, length, _ = x.shape
    t = n * length
    per_token = length < TOKEN_BLOCK
    rows = max(length, SUBLANES)
    x = x.reshape(t, D_MODEL)

    pos = pos0 + jnp.arange(rows, dtype=F32)
    half = RET_DK // 2
    inv = 1.0 / (ROPE_BASE ** jnp.linspace(0.0, 1.0, half, dtype=F32))
    ang = pos[:, None] * inv[None, :]
    cos = jnp.concatenate([jnp.cos(ang), jnp.cos(ang)], axis=-1)
    sin = jnp.concatenate([-jnp.sin(ang), jnp.sin(ang)], axis=-1)
    log_gamma = jnp.log(1.0 - 2.0 ** (-5.0 - jnp.arange(RET_HEADS, dtype=F32)))
    sm = jax.nn.softmax(w['hg_lower_bounds'].astype(F32), axis=0)
    lbs = jnp.cumsum(sm, axis=0) - sm[0]

    def mods(l):
        parts = jnp.split(mod[l], 6, axis=-1)
        if per_token:
            return [jnp.repeat(p, length, axis=0) for p in parts]
        return [p.reshape(n, 1, D_MODEL) for p in parts]

    new_ret, new_conv, new_hg = None, [], None
    for l in range(DEPTH):
        sh1, sc1, g1, sh2, sc2, g2 = mods(l)
        j = l // 2
        if l % 2 == 0:
            proj = _norm_mod_matmul(x, w['norm_mix_g'][l], sc1, sh1, w['ab_w_in'][j], per_token, length)
            projp = _pad_rows(proj, n, length, rows)
            ret_out, new_ret = _retention(projp, cos, sin, log_gamma, w['ret_norm_g'][j], ret_s, j,
                                          new_ret, n, rows, length)
            zprev8 = jnp.pad(conv_s[j], ((0, 0), (SUBLANES - 2, 0), (0, 0)))
            conv_out, tail = _conv(projp, w['conv_w'][j], zprev8, n, rows)
            r0 = (length - 2) % SUBLANES if rows == length else length - 2
            new_conv.append(tail[:, r0:r0 + 2, :])
            acts = [_unpad_rows(ret_out, n, length, rows), _unpad_rows(conv_out, n, length, rows)]
            wo = w['ab_w_out'][j]
            x = _outproj(x, g1, acts, [wo[:RET_WIDTH], wo[RET_WIDTH:]], per_token, length)
        else:
            proj = _norm_mod_matmul(x, w['norm_mix_g'][l], sc1, sh1, w['hg_w_in'][j], per_token, length)
            projp = _pad_rows(proj, n, length, rows)
            o, new_hg = _hgrn(projp, lbs[l], w['hg_norm_g'][j], hg_s, j, new_hg, n, rows, length)
            x = _outproj(x, g1, [_unpad_rows(o, n, length, rows)], [w['hg_w_out'][j]], per_token, length)
        e1, e2, gt = _route(x, w['norm_ffn_g'][l], sc2, sh2, w['peer_wq'][l], w['peer_keys'][l],
                            per_token, length)
        x = _experts(x, w['norm_ffn_g'][l], sc2, sh2, g2, e1, e2, gt, w['peer_u'], w['peer_v'], l,
                     per_token, length)
    y = _final_norm(x, w['final_norm_g']).reshape(n, length, D_MODEL)
    return y, new_ret, jnp.stack(new_conv), new_hg


def kernel(x_prompt, x_sample, state_ret, state_conv, state_hgrn, c_prompt, c_sample, ada_w, ada_b, norm_mix_g, norm_ffn_g, ab_w_in, ret_norm_g, conv_w, ab_w_out, hg_w_in, hg_lower_bounds, hg_norm_g, hg_w_out, peer_wq, peer_keys, peer_u, peer_v, final_norm_g):
    nb = x_prompt.shape[0]
    n_even, n_odd = state_ret.shape[0], state_hgrn.shape[0]
    w = dict(
        norm_mix_g=norm_mix_g, norm_ffn_g=norm_ffn_g, ret_norm_g=ret_norm_g, conv_w=conv_w,
        hg_lower_bounds=hg_lower_bounds, hg_norm_g=hg_norm_g, final_norm_g=final_norm_g,
        ab_w_in=ab_w_in.astype(BF16), ab_w_out=ab_w_out.astype(BF16),
        hg_w_in=hg_w_in.astype(BF16), hg_w_out=hg_w_out.astype(BF16),
        peer_wq=peer_wq.astype(BF16),
        peer_keys=peer_keys.astype(BF16).reshape(DEPTH, 2 * PEER_HEADS, N_KEYS, N_KEYS),
        peer_u=peer_u.astype(BF16), peer_v=peer_v.astype(BF16),
    )
    mod = _ada(jnp.concatenate([c_prompt, c_sample], axis=0), ada_w, ada_b)
    mod_p, mod_s = mod[:, :nb], mod[:, nb:]
    dt = x_prompt.dtype
    ret0 = jnp.zeros((n_even, nb) + state_ret.shape[2:], dt)
    conv0 = jnp.zeros((n_even, nb) + state_conv.shape[2:], dt)
    hg0 = jnp.zeros((n_odd, nb) + state_hgrn.shape[2:], dt)
    y_p, ret_p, conv_p, hg_p = _trunk(x_prompt, mod_p, 0.0, ret0, conv0, hg0, w)
    y_s, ret_s, conv_s, hg_s = _trunk(x_sample, mod_s, float(PAST_LEN), state_ret, state_conv, state_hgrn, w)
    return (y_p, y_s, ret_p, conv_p, hg_p, ret_s, conv_s, hg_s)
```

```python
import functools
import math

import jax
import jax.numpy as jnp
from jax import lax
from jax.experimental import pallas as pl
from jax.experimental.pallas import tpu as pltpu

F32 = jnp.float32
BF16 = jnp.bfloat16
I32 = jnp.int32
U32 = jnp.uint32

D_MODEL = 1024
DEPTH = 4
PAST_LEN = 16384
RET_HEADS = 4
RET_DK = 128
RET_WIDTH = RET_HEADS * RET_DK
RET_CHUNK = 128
ROPE_BASE = 10000.0
CONV_DIM = D_MODEL // 2
HG_HEADS = 8
HG_DK = 128
HG_SUB = 16
HG_BLOCK = 256
PEER_HEADS = 8
N_KEYS = 128
PEER_TOPK = 16
NORM_EPS = 1e-6

LANES = 128
SUBLANES = 8
VMEM_LIMIT = 56 * 1024 * 1024

TOKEN_BLOCK = 512
ROUTE_BLOCK = 256
ROUTE_HEADS = 4
EXPERT_ROWS = 512
EXPERT_CHUNK = 2048
W_PITCH = EXPERT_ROWS + SUBLANES


def _cparams(sem):
    return pltpu.CompilerParams(dimension_semantics=sem, vmem_limit_bytes=VMEM_LIMIT)


def _mod_spec(per_token, tm, rows_per_seq, block_of=lambda i: i):
    if per_token:
        return pl.BlockSpec((tm, D_MODEL), lambda i, *_: (block_of(i), 0))
    return pl.BlockSpec((None, 1, D_MODEL), lambda i, *_: (block_of(i) * tm // rows_per_seq, 0, 0))


def _norm_mod(x, g, sc, sh):
    y = x * lax.rsqrt(jnp.mean(x * x, axis=-1, keepdims=True) + NORM_EPS)
    return (y * g) * (1.0 + sc) + sh


def _silu(x):
    return x * (1.0 / (1.0 + jnp.exp(-x)))


def _sigmoid(x):
    return 1.0 / (1.0 + jnp.exp(-x))


def _ada_kernel(c_ref, w_ref, b_ref, o_ref):
    a = _silu(c_ref[...]).astype(BF16)
    o_ref[...] = jnp.dot(a, w_ref[...].astype(BF16), preferred_element_type=F32) + b_ref[...]


def _ada(c, ada_w, ada_b):
    n = c.shape[0]
    tn = 1536
    nj = 6 * D_MODEL // tn
    return pl.pallas_call(
        _ada_kernel,
        grid=(DEPTH, nj),
        in_specs=[
            pl.BlockSpec((n, D_MODEL), lambda l, j: (0, 0)),
            pl.BlockSpec((None, D_MODEL, tn), lambda l, j: (l, 0, j)),
            pl.BlockSpec((None, 1, tn), lambda l, j: (l, 0, j)),
        ],
        out_specs=pl.BlockSpec((None, n, tn), lambda l, j: (l, 0, j)),
        out_shape=jax.ShapeDtypeStruct((DEPTH, n, 6 * D_MODEL), F32),
        compiler_params=_cparams(("parallel", "parallel")),
        name="ada_mod",
    )(c, ada_w, ada_b.reshape(DEPTH, 1, 6 * D_MODEL))


def _nmm_kernel(x_ref, g_ref, sc_ref, sh_ref, w_ref, o_ref, *, tn):
    h = _norm_mod(x_ref[...], g_ref[...], sc_ref[...], sh_ref[...]).astype(BF16)
    for j in range(w_ref.shape[1] // tn):
        cols = slice(j * tn, (j + 1) * tn)
        o_ref[:, cols] = jnp.dot(h, w_ref[:, cols], preferred_element_type=F32)


def _norm_mod_matmul(x, g, sc, sh, w, per_token, rows_per_seq):
    t = x.shape[0]
    f = w.shape[1]
    tm = min(TOKEN_BLOCK, t)
    mspec = _mod_spec(per_token, tm, rows_per_seq)
    return pl.pallas_call(
        functools.partial(_nmm_kernel, tn=512),
        grid=(t // tm,),
        in_specs=[
            pl.BlockSpec((tm, D_MODEL), lambda i: (i, 0)),
            pl.BlockSpec((1, D_MODEL), lambda i: (0, 0)),
            mspec, mspec,
            pl.BlockSpec((D_MODEL, f), lambda i: (0, 0)),
        ],
        out_specs=pl.BlockSpec((tm, f), lambda i: (i, 0)),
        out_shape=jax.ShapeDtypeStruct((t, f), F32),
        compiler_params=_cparams(("parallel",)),
        name="norm_mod_proj",
    )(x, g.reshape(1, D_MODEL), sc, sh, w)


def _ret_kernel(lg_ref, q_ref, k_ref, v_ref, g_ref, cos_ref, sin_ref, rg_ref, s0_ref, acc_ref,
                o_ref, sout_ref, s_scr, *, chunk, real):
    del acc_ref
    c = pl.program_id(1)

    @pl.when(c == 0)
    def _():
        s_scr[...] = s0_ref[...]

    cos = cos_ref[...]
    sin = sin_ref[...]

    def rope(x):
        return x * cos + pltpu.roll(x, RET_DK // 2, 1) * sin

    ri = lax.broadcasted_iota(I32, (chunk, chunk), 0)
    ci = lax.broadcasted_iota(I32, (chunk, chunk), 1)
    causal = ri >= ci
    rel = jnp.where(causal, ri - ci, 0).astype(F32)
    pos = lax.broadcasted_iota(I32, (chunk, RET_DK), 0).astype(F32)

    for hh in range(RET_HEADS):
        cols = slice(hh * RET_DK, (hh + 1) * RET_DK)
        lg = lg_ref[hh]
        decay = jnp.where(causal, jnp.exp(lg * rel), 0.0)
        in_decay = jnp.exp(lg * (pos + 1.0))
        st_decay = jnp.exp(lg * (real - 1.0 - pos))
        ch_decay = jnp.exp(jnp.full((1, RET_DK), real, F32) * lg)
        q = rope(q_ref[:, cols])
        k = rope(k_ref[:, cols]) * (RET_DK ** -0.5)
        v = v_ref[:, cols]
        s = s_scr[hh]
        att = lax.dot_general(q, k, (((1,), (1,)), ((), ())), preferred_element_type=F32) * decay
        o = (jnp.dot(att, v, preferred_element_type=F32)
             + jnp.dot(q * in_decay, s, preferred_element_type=F32))
        s_scr[hh] = s * ch_decay + lax.dot_general(k * st_decay, v, (((0,), (0,)), ((), ())),
                                                   preferred_element_type=F32)
        y = o * lax.rsqrt(jnp.mean(o * o, axis=-1, keepdims=True) + NORM_EPS) * rg_ref[:, cols]
        o_ref[:, cols] = _silu(g_ref[:, cols]) * y

    @pl.when(c == pl.num_programs(1) - 1)
    def _():
        sout_ref[...] = s_scr[...]


def _state_io(states, layer, prev_out, n_leading_inputs):
    spec = pl.BlockSpec((None, None) + states.shape[2:], lambda b, c, *_: (layer, b, 0, 0, 0))
    return spec, [pl.BlockSpec(memory_space=pl.ANY)], [prev_out], {n_leading_inputs: 1}


def _retention(proj, cos, sin, log_gamma, ret_g, states, layer, prev_out, n, rows, real):
    chunk = min(rows, RET_CHUNK)
    nc = rows // chunk
    h = RET_HEADS

    def col(j):
        return pl.BlockSpec((chunk, RET_WIDTH), lambda b, c, *_: (b * nc + c, j))

    tab = pl.BlockSpec((chunk, RET_DK), lambda b, c, *_: (c, 0))
    st, extra_specs, extra_args, aliases = _state_io(states, layer, prev_out, 9)
    grid_spec = pltpu.PrefetchScalarGridSpec(
        num_scalar_prefetch=1,
        grid=(n, nc),
        in_specs=[col(0), col(1), col(2), col(3), tab, tab,
                  pl.BlockSpec((1, RET_WIDTH), lambda b, c, *_: (0, 0)), st] + extra_specs,
        out_specs=[col(0), st],
        scratch_shapes=[pltpu.VMEM((h, RET_DK, RET_DK), F32)],
    )
    return pl.pallas_call(
        functools.partial(_ret_kernel, chunk=chunk, real=float(real if nc == 1 else chunk)),
        grid_spec=grid_spec,
        out_shape=[jax.ShapeDtypeStruct((n * rows, RET_WIDTH), F32),
                   jax.ShapeDtypeStruct(states.shape, F32)],
        input_output_aliases=aliases,
        compiler_params=_cparams(("parallel", "arbitrary")),
        name="retention",
    )(log_gamma, proj, proj, proj, proj, cos, sin, ret_g.reshape(1, RET_WIDTH), states, *extra_args)


def _conv_kernel(bg_ref, cg_ref, xi_ref, w_ref, zp_ref, o_ref, tail_ref, prev_scr, *, chunk):
    c = pl.program_id(1)

    @pl.when(c == 0)
    def _():
        prev_scr[...] = zp_ref[...]

    z = cg_ref[...] * xi_ref[...]
    p2 = prev_scr[SUBLANES - 2:SUBLANES - 1, :]
    p1 = prev_scr[SUBLANES - 1:SUBLANES, :]
    row = lax.broadcasted_iota(I32, (chunk, CONV_DIM), 0)
    z1 = jnp.where(row == 0, p1, pltpu.roll(z, 1, 0))
    z2 = jnp.where(row == 0, p2, jnp.where(row == 1, p1, pltpu.roll(z, 2, 0)))
    conv = w_ref[0:1, :] * z2 + w_ref[1:2, :] * z1 + w_ref[2:3, :] * z
    o_ref[...] = bg_ref[...] * conv
    tail = z[chunk - SUBLANES:, :]
    prev_scr[...] = tail
    tail_ref[...] = tail


def _conv(proj, conv_w, zprev8, n, rows):
    chunk = min(rows, TOKEN_BLOCK)
    nc = rows // chunk
    base = 4 * RET_WIDTH // CONV_DIM

    def col(j):
        return pl.BlockSpec((chunk, CONV_DIM), lambda b, c: (b * nc + c, base + j))

    return pl.pallas_call(
        functools.partial(_conv_kernel, chunk=chunk),
        grid=(n, nc),
        in_specs=[col(0), col(1), col(2),
                  pl.BlockSpec((3, CONV_DIM), lambda b, c: (0, 0)),
                  pl.BlockSpec((None, SUBLANES, CONV_DIM), lambda b, c: (b, 0, 0))],
        out_specs=[pl.BlockSpec((chunk, CONV_DIM), lambda b, c: (b * nc + c, 0)),
                   pl.BlockSpec((None, SUBLANES, CONV_DIM), lambda b, c: (b, 0, 0))],
        out_shape=[jax.ShapeDtypeStruct((n * rows, CONV_DIM), F32),
                   jax.ShapeDtypeStruct((n, SUBLANES, CONV_DIM), F32)],
        scratch_shapes=[pltpu.VMEM((SUBLANES, CONV_DIM), F32)],
        compiler_params=_cparams(("parallel", "arbitrary")),
        name="short_conv",
    )(proj, proj, proj, conv_w, zprev8)


def _hgrn_kernel(q_ref, f_ref, i_ref, g_ref, lb_ref, ng_ref, s0_ref, acc_ref, o_ref, sout_ref,
                 st_scr, b_scr, q_scr, k_scr, o_scr, *, block, sub, real):
    del acc_ref
    c = pl.program_id(1)

    @pl.when(c == 0)
    def _():
        for hh in range(HG_HEADS):
            st_scr[hh] = s0_ref[hh].T

    lb = lb_ref[...]
    fg = lb + (1.0 - lb) * _sigmoid(f_ref[...])
    k_scr[...] = 1.0 - fg
    q_scr[...] = _silu(q_ref[...])
    b = jnp.log(fg)
    rowm = lax.broadcasted_iota(I32, b.shape, 0) & (sub - 1)
    step = 1
    while step < sub:
        b = b + jnp.where(rowm >= step, pltpu.roll(b, step, 0), 0.0)
        step *= 2
    b_scr[...] = b

    ones = jnp.ones((HG_DK, HG_DK), BF16)
    rowi = lax.broadcasted_iota(I32, (SUBLANES, HG_DK), 0)
    nsub = block // sub
    last = (real - 1) if nsub == 1 else (sub - 1)

    def body(sb, carry):
        rows = pl.ds(pl.multiple_of(sb * sub, sub), sub)
        for hh in range(HG_HEADS):
            cols = slice(hh * HG_DK, (hh + 1) * HG_DK)
            b = b_scr[rows, cols]
            q = q_scr[rows, cols]
            kk = k_scr[rows, cols]
            iv = i_ref[rows, cols]
            st = st_scr[hh]
            o = lax.dot_general((q * jnp.exp(b)).astype(BF16), st.astype(BF16),
                                (((1,), (1,)), ((), ())), preferred_element_type=F32)
            parts = []
            for j in range(sub):
                for grp in range(j // SUBLANES, sub // SUBLANES):
                    r = slice(grp * SUBLANES, (grp + 1) * SUBLANES)
                    d = b[r, :] - b[j:j + 1, :]
                    if grp == j // SUBLANES:
                        d = jnp.where(rowi >= j - grp * SUBLANES, d, -jnp.inf)
                    parts.append(q[r, :] * kk[j:j + 1, :] * jnp.exp(d))
            t_all = jnp.concatenate(parts, axis=0).astype(BF16)
            s_all = jnp.dot(t_all, ones, preferred_element_type=F32)
            groups = [o[g * SUBLANES:(g + 1) * SUBLANES, :] for g in range(sub // SUBLANES)]
            n = 0
            for j in range(sub):
                for grp in range(j // SUBLANES, sub // SUBLANES):
                    groups[grp] = groups[grp] + s_all[n * SUBLANES:(n + 1) * SUBLANES, :] * iv[j:j + 1, :]
                    n += 1
            o = jnp.concatenate(groups, axis=0)
            b_end = b[last:last + 1, :]
            kd = (kk * jnp.exp(b_end - b)).astype(BF16)
            st_scr[hh] = st * jnp.exp(b_end) + lax.dot_general(
                iv.astype(BF16), kd, (((0,), (0,)), ((), ())), preferred_element_type=F32)
            o_scr[rows, cols] = o
        return carry

    lax.fori_loop(0, nsub, body, 0)

    for hh in range(HG_HEADS):
        cols = slice(hh * HG_DK, (hh + 1) * HG_DK)
        o = o_scr[:, cols]
        y = o * lax.rsqrt(jnp.mean(o * o, axis=-1, keepdims=True) + NORM_EPS) * ng_ref[:, cols]
        o_ref[:, cols] = y * _silu(g_ref[:, cols])

    @pl.when(c == pl.num_programs(1) - 1)
    def _():
        for hh in range(HG_HEADS):
            sout_ref[hh] = st_scr[hh].T


def _hgrn(proj, lb, norm_g, states, layer, prev_out, n, rows, real):
    block = min(rows, HG_BLOCK)
    nc = rows // block
    sub = min(HG_SUB, block)
    h = HG_HEADS
    width = h * HG_DK

    def col(j):
        return pl.BlockSpec((block, width), lambda b, c: (b * nc + c, j))

    vec = pl.BlockSpec((1, width), lambda b, c: (0, 0))
    st, extra_specs, extra_args, aliases = _state_io(states, layer, prev_out, 7)
    return pl.pallas_call(
        functools.partial(_hgrn_kernel, block=block, sub=sub, real=real),
        grid=(n, nc),
        in_specs=[col(0), col(1), col(2), col(3), vec, vec, st] + extra_specs,
        out_specs=[col(0), st],
        out_shape=[jax.ShapeDtypeStruct((n * rows, width), F32),
                   jax.ShapeDtypeStruct(states.shape, F32)],
        input_output_aliases=aliases,
        scratch_shapes=[pltpu.VMEM((h, HG_DK, HG_DK), F32)] + [pltpu.VMEM((block, width), F32)] * 4,
        compiler_params=_cparams(("parallel", "arbitrary")),
        name="hgrn2",
    )(proj, proj, proj, proj, lb.reshape(1, width), norm_g.reshape(1, width), states, *extra_args)


def _outproj_kernel(*refs, n_in):
    x_ref, gate_ref = refs[0], refs[1]
    a_refs = refs[2:2 + n_in]
    w_refs = refs[2 + n_in:2 + 2 * n_in]
    o_ref = refs[2 + 2 * n_in]
    y = jnp.dot(a_refs[0][...].astype(BF16), w_refs[0][...], preferred_element_type=F32)
    for a_ref, w_ref in zip(a_refs[1:], w_refs[1:]):
        y = y + jnp.dot(a_ref[...].astype(BF16), w_ref[...], preferred_element_type=F32)
    o_ref[...] = x_ref[...] + gate_ref[...] * y


def _outproj(x, gate, acts, ws, per_token, rows_per_seq):
    t = x.shape[0]
    tm = min(TOKEN_BLOCK, t)
    n_in = len(acts)
    in_specs = [pl.BlockSpec((tm, D_MODEL), lambda i: (i, 0)), _mod_spec(per_token, tm, rows_per_seq)]
    in_specs += [pl.BlockSpec((tm, a.shape[1]), lambda i: (i, 0)) for a in acts]
    in_specs += [pl.BlockSpec(w.shape, lambda i: (0, 0)) for w in ws]
    return pl.pallas_call(
        functools.partial(_outproj_kernel, n_in=n_in),
        grid=(t // tm,),
        in_specs=in_specs,
        out_specs=pl.BlockSpec((tm, D_MODEL), lambda i: (i, 0)),
        out_shape=jax.ShapeDtypeStruct((t, D_MODEL), F32),
        compiler_params=_cparams(("parallel",)),
        name="out_proj",
    )(x, gate, *acts, *ws)


def _load(ref):
    return ref[...]


def _topk16(problems):
    def body(k, carry):
        for v_ref, code_fn, val_out, code_out in problems:
            v = v_ref[...]
            code = code_fn()
            m = jnp.max(v, axis=0, keepdims=True)
            cmin = jnp.min(jnp.where(v == m, code, jnp.inf), axis=0, keepdims=True)
            v_ref[...] = jnp.where(code == cmin, -jnp.inf, v)
            val_out[pl.ds(k, 1), :] = m
            code_out[pl.ds(k, 1), :] = cmin
        return carry

    lax.fori_loop(0, PEER_TOPK, body, 0)


def _topk16_paired(problems):
    def body(k, carry):
        for hi_ref, lo_ref, chi_ref, clo_ref, val_out, code_out in problems:
            hi = hi_ref[...]
            chi = chi_ref[...]
            lo = lo_ref[...]
            m = jnp.max(hi, axis=0, keepdims=True)
            cmin = jnp.min(jnp.where(hi == m, chi, jnp.inf), axis=0, keepdims=True)
            took = chi == cmin
            hi_ref[...] = jnp.where(took, lo, hi)
            chi_ref[...] = jnp.where(took, clo_ref[...], chi)
            lo_ref[...] = jnp.where(took, -jnp.inf, lo)
            val_out[pl.ds(k, 1), :] = m
            code_out[pl.ds(k, 1), :] = cmin
        return carry

    lax.fori_loop(0, PEER_TOPK, body, 0)


def _route_kernel(x_ref, g_ref, sc_ref, sh_ref, wq_ref, keys_ref, e1_ref, e2_ref, gt_ref,
                  q_scr, hi_scr, lo_scr, chi_scr, clo_scr, comb_scr, code_scr, v_scr, i_scr,
                  vs_scr, cs_scr, e1t_scr, e2t_scr, gtt_scr, *, tr):
    h = _norm_mod(x_ref[...], g_ref[...], sc_ref[...], sh_ref[...]).astype(BF16)
    q = jnp.dot(h, wq_ref[...], preferred_element_type=F32).astype(BF16)
    for hp in range(2 * PEER_HEADS):
        q_scr[hp] = q[:, hp * N_KEYS:(hp + 1) * N_KEYS]

    kk = PEER_TOPK
    nexp = float(N_KEYS * N_KEYS)
    half = N_KEYS // 2
    key_lo = lax.broadcasted_iota(I32, (half, tr), 0).astype(F32)

    def heads(it, carry):
        first = []
        for u in range(2 * ROUTE_HEADS):
            hp = 2 * ROUTE_HEADS * it + u
            s = lax.dot_general(keys_ref[hp], q_scr[hp], (((1,), (1,)), ((), ())),
                                preferred_element_type=F32)
            a, b = s[:half, :], s[half:, :]
            ge = a >= b
            hi_scr[u] = jnp.where(ge, a, b)
            lo_scr[u] = jnp.where(ge, b, a)
            chi_scr[u] = jnp.where(ge, key_lo, key_lo + half)
            clo_scr[u] = jnp.where(ge, key_lo + half, key_lo)
            first.append((hi_scr.at[u], lo_scr.at[u], chi_scr.at[u], clo_scr.at[u],
                          v_scr.at[u], i_scr.at[u]))
        _topk16_paired(first)
        second = []
        for u in range(ROUTE_HEADS):
            v1_ref, i1_ref = v_scr.at[2 * u], i_scr.at[2 * u]
            v2 = v_scr[2 * u + 1]
            i2 = i_scr[2 * u + 1]
            comb, code = comb_scr.at[u], code_scr.at[u]
            off = 0
            for a in range(kk // 2):
                nb = kk // (a + 1)
                rows = -(-nb // SUBLANES) * SUBLANES
                bpos = lax.broadcasted_iota(I32, (rows, tr), 0)
                val = v1_ref[a:a + 1, :] + v2[0:rows, :]
                if nb < rows:
                    val = jnp.where(bpos < nb, val, -jnp.inf)
                comb[off:off + rows, :] = val
                code[off:off + rows, :] = ((bpos + a * kk).astype(F32) * nexp
                                           + (i1_ref[a:a + 1, :] * N_KEYS + i2[0:rows, :]))
                off += rows
            apos = lax.broadcasted_iota(I32, (kk // 2, tr), 0) + kk // 2
            comb[off:off + kk // 2, :] = v1_ref[kk // 2:kk, :] + v2[0:1, :]
            code[off:off + kk // 2, :] = ((apos * kk).astype(F32) * nexp
                                          + (i1_ref[kk // 2:kk, :] * N_KEYS + i2[0:1, :]))
            second.append((comb, functools.partial(_load, code), vs_scr.at[u], cs_scr.at[u]))
        _topk16(second)
        for u in range(ROUTE_HEADS):
            vals = vs_scr[u]
            ex = jnp.exp(vals - vals[0:1, :])
            gate = ex / jnp.sum(ex, axis=0, keepdims=True)
            eid = cs_scr[u].astype(I32) & (N_KEYS * N_KEYS - 1)
            rows = pl.ds(pl.multiple_of((ROUTE_HEADS * it + u) * kk, kk), kk)
            e1t_scr[rows, :] = (eid >> 7).astype(F32)
            e2t_scr[rows, :] = (eid & (N_KEYS - 1)).astype(F32)
            gtt_scr[rows, :] = gate
        return carry

    lax.fori_loop(0, PEER_HEADS // ROUTE_HEADS, heads, 0)
    e1_ref[...] = e1t_scr[...].T
    e2_ref[...] = e2t_scr[...].T
    gt_ref[...] = gtt_scr[...].T


def _route(x, g, sc, sh, wq, keys, per_token, rows_per_seq):
    t = x.shape[0]
    tr = min(ROUTE_BLOCK, t)
    nslot = PEER_HEADS * PEER_TOPK
    mspec = _mod_spec(per_token, tr, rows_per_seq)
    out = pl.BlockSpec((tr, nslot), lambda i: (i, 0))
    kk = PEER_TOPK
    rh = ROUTE_HEADS
    ncand = sum(-(-(kk // (a + 1)) // SUBLANES) * SUBLANES for a in range(kk // 2)) + kk // 2
    return pl.pallas_call(
        functools.partial(_route_kernel, tr=tr),
        grid=(t // tr,),
        in_specs=[pl.BlockSpec((tr, D_MODEL), lambda i: (i, 0)),
                  pl.BlockSpec((1, D_MODEL), lambda i: (0, 0)),
                  mspec, mspec,
                  pl.BlockSpec(wq.shape, lambda i: (0, 0)),
                  pl.BlockSpec(keys.shape, lambda i: (0, 0, 0))],
        out_specs=[out, out, out],
        out_shape=[jax.ShapeDtypeStruct((t, nslot), F32)] * 3,
        scratch_shapes=[pltpu.VMEM((2 * PEER_HEADS, tr, N_KEYS), BF16)]
                       + [pltpu.VMEM((2 * rh, N_KEYS // 2, tr), F32)] * 4
                       + [pltpu.VMEM((rh, ncand, tr), F32)] * 2
                       + [pltpu.VMEM((2 * rh, kk, tr), F32)] * 2
                       + [pltpu.VMEM((rh, kk, tr), F32)] * 2
                       + [pltpu.VMEM((nslot, tr), F32)] * 3,
        compiler_params=_cparams(("parallel",)),
        name="peer_route",
    )(x, g.reshape(1, D_MODEL), sc, sh, wq, keys)


def _gelu(x):
    return 0.5 * x * (1.0 + lax.erf(x * (2.0 ** -0.5)))


def _experts_kernel(x_ref, g_ref, sc_ref, sh_ref, gg_ref, fg_ref, e1_ref, e2_ref, gt_ref, u_ref,
                    v_ref, o_ref, h_scr, w_scr, acc_scr, p_scr, *, ts, final):
    c = pl.program_id(1)
    nsl = N_KEYS // 2
    chunks_per_half = nsl // (2 * SUBLANES)

    @pl.when(c == 0)
    def _():
        h_scr[...] = _norm_mod(x_ref[...], g_ref[...], sc_ref[...], sh_ref[...]).astype(BF16)
        acc_scr[...] = jnp.zeros(acc_scr.shape, F32)

    @pl.when(c % chunks_per_half == 0)
    def _():
        key1 = (lax.broadcasted_iota(I32, (nsl, N_KEYS), 0) + (c // chunks_per_half) * nsl).astype(F32)
        key2 = lax.broadcasted_iota(I32, (N_KEYS, N_KEYS), 0).astype(F32)

        def body(tb, carry):
            base = pl.multiple_of(tb * SUBLANES, SUBLANES)
            e1b = e1_ref[pl.ds(base, SUBLANES), :]
            e2b = e2_ref[pl.ds(base, SUBLANES), :]
            gtb = gt_ref[pl.ds(base, SUBLANES), :]
            for i in range(SUBLANES):
                r1 = jnp.where(key1 == e1b[i:i + 1, :], 1.0, 0.0).astype(BF16)
                r2 = jnp.where(key2 == e2b[i:i + 1, :], gtb[i:i + 1, :], 0.0).astype(BF16)
                wt = lax.dot_general(r1, r2, (((1,), (1,)), ((), ())), preferred_element_type=F32)
                w_scr[pl.ds(base + i, nsl, stride=W_PITCH), :] = wt
            return carry

        lax.fori_loop(0, ts // SUBLANES, body, 0, unroll=8)

    hmat = h_scr[...]
    for s in range(SUBLANES):
        a2 = lax.dot_general(hmat, u_ref[s * 256:(s + 1) * 256, :], (((1,), (1,)), ((), ())),
                             preferred_element_type=F32)
        start = pl.multiple_of(((c % chunks_per_half) * 2 * SUBLANES + 2 * s) * W_PITCH, SUBLANES)
        w2 = jnp.concatenate([w_scr[pl.ds(start, ts), :], w_scr[pl.ds(start + W_PITCH, ts), :]], axis=1)
        p_scr[:, s * 256:(s + 1) * 256] = (_gelu(a2) * w2).astype(BF16)
    acc_scr[...] += jnp.dot(p_scr[...], v_ref[...], preferred_element_type=F32)

    @pl.when(c == pl.num_programs(1) - 1)
    def _():
        y = x_ref[...] + gg_ref[...] * acc_scr[...]
        if final:
            y = y * lax.rsqrt(jnp.mean(y * y, axis=-1, keepdims=True) + NORM_EPS) * fg_ref[...]
        o_ref[...] = y


def _experts(x, g, sc, sh, gg, final_g, e1, e2, gt, u, v, layer, final, per_token, rows_per_seq):
    t = x.shape[0]
    ts = EXPERT_ROWS
    assert t % ts == 0 and EXPERT_CHUNK == 2 * SUBLANES * N_KEYS
    nslot = PEER_HEADS * PEER_TOPK
    mspec = _mod_spec(per_token, ts, rows_per_seq)
    slot = pl.BlockSpec((ts, nslot), lambda i, c: (i, 0))
    tab = pl.BlockSpec((None, EXPERT_CHUNK, D_MODEL), lambda i, c: (layer, c, 0))
    vec = pl.BlockSpec((1, D_MODEL), lambda i, c: (0, 0))
    return pl.pallas_call(
        functools.partial(_experts_kernel, ts=ts, final=final),
        grid=(t // ts, N_KEYS * N_KEYS // EXPERT_CHUNK),
        in_specs=[pl.BlockSpec((ts, D_MODEL), lambda i, c: (i, 0)),
                  vec, mspec, mspec, mspec, vec, slot, slot, slot, tab, tab],
        out_specs=pl.BlockSpec((ts, D_MODEL), lambda i, c: (i, 0)),
        out_shape=jax.ShapeDtypeStruct((t, D_MODEL), F32),
        scratch_shapes=[pltpu.VMEM((ts, D_MODEL), BF16),
                        pltpu.VMEM((N_KEYS // 2 * W_PITCH, LANES), F32),
                        pltpu.VMEM((ts, D_MODEL), F32),
                        pltpu.VMEM((ts, EXPERT_CHUNK), BF16)],
        compiler_params=_cparams(("parallel", "arbitrary")),
        name="peer_experts",
    )(x, g.reshape(1, D_MODEL), sc, sh, gg, final_g.reshape(1, D_MODEL), e1, e2, gt, u, v)


def _pad_rows(a, n, length, rows):
    if rows == length:
        return a
    a = a.reshape(n, length, a.shape[-1])
    a = jnp.pad(a, ((0, 0), (0, rows - length), (0, 0)))
    return a.reshape(n * rows, a.shape[-1])


def _unpad_rows(a, n, length, rows):
    if rows == length:
        return a
    return a.reshape(n, rows, a.shape[-1])[:, :length].reshape(n * length, a.shape[-1])


def _trunk(x, mod, pos0, ret_s, conv_s, hg_s, w):
    n, length, _ = x.shape
    t = n * length
    per_token = length < TOKEN_BLOCK
    rows = max(length, SUBLANES)
    x = x.reshape(t, D_MODEL)

    pos = pos0 + jnp.arange(rows, dtype=F32)
    half = RET_DK // 2
    inv = 1.0 / (ROPE_BASE ** jnp.linspace(0.0, 1.0, half, dtype=F32))
    ang = pos[:, None] * inv[None, :]
    cos = jnp.concatenate([jnp.cos(ang), jnp.cos(ang)], axis=-1)
    sin = jnp.concatenate([-jnp.sin(ang), jnp.sin(ang)], axis=-1)
    log_gamma = jnp.log(1.0 - 2.0 ** (-5.0 - jnp.arange(RET_HEADS, dtype=F32)))
    sm = jax.nn.softmax(w['hg_lower_bounds'].astype(F32), axis=0)
    lbs = jnp.cumsum(sm, axis=0) - sm[0]

    def mods(l):
        parts = jnp.split(mod[l], 6, axis=-1)
        if per_token:
            return [jnp.repeat(p, length, axis=0) for p in parts]
        return [p.reshape(n, 1, D_MODEL) for p in parts]

    new_ret, new_conv, new_hg = jnp.zeros_like(ret_s), [], jnp.zeros_like(hg_s)
    for l in range(DEPTH):
        sh1, sc1, g1, sh2, sc2, g2 = mods(l)
        j = l // 2
        if l % 2 == 0:
            proj = _norm_mod_matmul(x, w['norm_mix_g'][l], sc1, sh1, w['ab_w_in'][j], per_token, length)
            projp = _pad_rows(proj, n, length, rows)
            ret_out, new_ret = _retention(projp, cos, sin, log_gamma, w['ret_norm_g'][j], ret_s, j,
                                          new_ret, n, rows, length)
            zprev8 = jnp.pad(conv_s[j], ((0, 0), (SUBLANES - 2, 0), (0, 0)))
            conv_out, tail = _conv(projp, w['conv_w'][j], zprev8, n, rows)
            r0 = (length - 2) % SUBLANES if rows == length else length - 2
            new_conv.append(tail[:, r0:r0 + 2, :])
            acts = [_unpad_rows(ret_out, n, length, rows), _unpad_rows(conv_out, n, length, rows)]
            wo = w['ab_w_out'][j]
            x = _outproj(x, g1, acts, [wo[:RET_WIDTH], wo[RET_WIDTH:]], per_token, length)
        else:
            proj = _norm_mod_matmul(x, w['norm_mix_g'][l], sc1, sh1, w['hg_w_in'][j], per_token, length)
            projp = _pad_rows(proj, n, length, rows)
            o, new_hg = _hgrn(projp, lbs[l], w['hg_norm_g'][j], hg_s, j, new_hg, n, rows, length)
            x = _outproj(x, g1, [_unpad_rows(o, n, length, rows)], [w['hg_w_out'][j]], per_token, length)
        e1, e2, gt = _route(x, w['norm_ffn_g'][l], sc2, sh2, w['peer_wq'][l], w['peer_keys'][l],
                            per_token, length)
        x = _experts(x, w['norm_ffn_g'][l], sc2, sh2, g2, w['final_norm_g'], e1, e2, gt,
                     w['peer_u'], w['peer_v'], l, l == DEPTH - 1, per_token, length)
    return x.reshape(n, length, D_MODEL), new_ret, jnp.stack(new_conv), new_hg


def kernel(x_prompt, x_sample, state_ret, state_conv, state_hgrn, c_prompt, c_sample, ada_w, ada_b, norm_mix_g, norm_ffn_g, ab_w_in, ret_norm_g, conv_w, ab_w_out, hg_w_in, hg_lower_bounds, hg_norm_g, hg_w_out, peer_wq, peer_keys, peer_u, peer_v, final_norm_g):
    nb = x_prompt.shape[0]
    n_even, n_odd = state_ret.shape[0], state_hgrn.shape[0]
    w = dict(
        norm_mix_g=norm_mix_g, norm_ffn_g=norm_ffn_g, ret_norm_g=ret_norm_g, conv_w=conv_w,
        hg_lower_bounds=hg_lower_bounds, hg_norm_g=hg_norm_g, final_norm_g=final_norm_g,
        ab_w_in=ab_w_in.astype(BF16), ab_w_out=ab_w_out.astype(BF16),
        hg_w_in=hg_w_in.astype(BF16), hg_w_out=hg_w_out.astype(BF16),
        peer_wq=peer_wq.astype(BF16),
        peer_keys=peer_keys.astype(BF16).reshape(DEPTH, 2 * PEER_HEADS, N_KEYS, N_KEYS),
        peer_u=peer_u.astype(BF16), peer_v=peer_v.astype(BF16),
    )
    mod = _ada(jnp.concatenate([c_prompt, c_sample], axis=0), ada_w, ada_b)
    mod_p, mod_s = mod[:, :nb], mod[:, nb:]
    dt = x_prompt.dtype
    ret0 = jnp.zeros((n_even, nb) + state_ret.shape[2:], dt)
    conv0 = jnp.zeros((n_even, nb) + state_conv.shape[2:], dt)
    hg0 = jnp.zeros((n_odd, nb) + state_hgrn.shape[2:], dt)
    y_p, ret_p, conv_p, hg_p = _trunk(x_prompt, mod_p, 0.0, ret0, conv0, hg0, w)
    y_s, ret_s, conv_s, hg_s = _trunk(x_sample, mod_s, float(PAST_LEN), state_ret, state_conv, state_hgrn, w)
    return (y_p, y_s, ret_p, conv_p, hg_p, ret_s, conv_s, hg_s)
```
